```python
import math
import jax, jax.numpy as jnp
from jax import lax
import numpy as np

D_MODEL = 2048
BATCH = 4
SEQ = 2048
DEPTH = 4
DEC_BATCH = 128
DEC_SEQ = 1
PAST_LEN = 16384
PAGE_SIZE = 128

RET_WIDTH = D_MODEL // 2
RET_HEADS = 8
RET_HEAD_DIM = RET_WIDTH // RET_HEADS
RET_CHUNK = 128
ROPE_BASE = 10000.0
RWKV_WIDTH = D_MODEL - RET_WIDTH
RWKV_HEAD_DIM = 64
RWKV_HEADS = RWKV_WIDTH // RWKV_HEAD_DIM


def _lora_rank(mult, power, width):
    return max(32, int(round(mult * width ** power / 32)) * 32)


R_DECAY = _lora_rank(1.8, 0.5, RWKV_WIDTH)
R_AAA = _lora_rank(1.8, 0.5, RWKV_WIDTH)
R_MV = _lora_rank(1.3, 0.5, RWKV_WIDTH)
R_GATE = _lora_rank(0.6, 0.8, RWKV_WIDTH)
D_FF = ((8 * D_MODEL // 3 + 255) // 256) * 256

RET_COLS = 4 * RET_WIDTH
RWKV_COLS = 3 * RWKV_WIDTH + R_DECAY + R_AAA + R_GATE
IN_COLS = RET_COLS + RWKV_COLS
NORM_EPS = 1e-6
RET_LN_EPS = 1e-5
RWKV_LN_EPS = 64e-5

kernel_name = "hybrid_retention_rwkv7_step"


def rms_norm(x, g):
    xf = x.astype(jnp.float32)
    y = xf * lax.rsqrt(jnp.mean(xf * xf, axis=-1, keepdims=True) + NORM_EPS)
    return (y * g.astype(jnp.float32)).astype(x.dtype)


def head_norm(x, n_heads, eps):
    xf = x.astype(jnp.float32).reshape(x.shape[:-1] + (n_heads, -1))
    mu = jnp.mean(xf, axis=-1, keepdims=True)
    xc = xf - mu
    var = jnp.mean(xc * xc, axis=-1, keepdims=True)
    return (xc * lax.rsqrt(var + eps)).reshape(x.shape)


def rope(x, pos):
    half = x.shape[-1] // 2
    inv_freq = ROPE_BASE ** (-jnp.arange(half, dtype=jnp.float32) / half)
    ang = pos[:, None] * inv_freq[None, :]
    cos = jnp.cos(ang)[None, :, None, :]
    sin = jnp.sin(ang)[None, :, None, :]
    x1, x2 = x[..., :half], x[..., half:]
    return jnp.concatenate([x1 * cos - x2 * sin, x1 * sin + x2 * cos], axis=-1)


def retention_chunkwise(q, k, v, s0, log_gamma):
    b, t, h, d = q.shape
    c = t if t <= RET_CHUNK else math.gcd(t, RET_CHUNK)
    n = t // c
    q, k, v = (z.reshape(b, n, c, h, d) for z in (q, k, v))
    idx = jnp.arange(c, dtype=jnp.float32)
    diff = idx[:, None] - idx[None, :]
    decay_in = jnp.where(diff >= 0, jnp.exp(log_gamma[:, None, None] * jnp.maximum(diff, 0.0)), 0.0)
    scores = jnp.einsum('bnihd,bnjhd->bnhij', q, k) * decay_in
    y_intra = jnp.einsum('bnhij,bnjhe->bnihe', scores, v)
    k_tail = k * jnp.exp((c - 1 - idx)[:, None] * log_gamma[None, :])[:, :, None]
    kv_chunks = jnp.einsum('bnjhd,bnjhe->nbhde', k_tail, v)
    chunk_decay = jnp.exp(c * log_gamma)[None, :, None, None]

    def step(s, kv):
        return s * chunk_decay + kv, s

    s_final, s_before = lax.scan(step, s0, kv_chunks)
    q_head = q * jnp.exp((idx + 1)[:, None] * log_gamma[None, :])[:, :, None]
    y_inter = jnp.einsum('bnihd,nbhde->bnihe', q_head, s_before)
    return (y_intra + y_inter).reshape(b, t, h, d), s_final


def retention_mixer(p_ret, state, pos, ln_w, log_gamma):
    b, t, _ = p_ret.shape
    q, k, v, g = jnp.split(p_ret.astype(jnp.float32), 4, axis=-1)
    shp = (b, t, RET_HEADS, RET_HEAD_DIM)
    q = rope(q.reshape(shp), pos)
    k = rope(k.reshape(shp), pos) * (RET_HEAD_DIM ** -0.5)
    y, s_new = retention_chunkwise(q, k, v.reshape(shp), state.astype(jnp.float32), log_gamma)
    y = head_norm(y.reshape(b, t, RET_WIDTH), RET_HEADS, RET_LN_EPS) * ln_w.astype(jnp.float32)
    return (jax.nn.silu(g) * y).astype(p_ret.dtype), s_new


def rwkv7_scan(r, decay, k, v, kk, a, s0):
    def step(s, inp):
        r_t, w_t, k_t, v_t, kk_t, a_t = inp
        sa = jnp.einsum('bhvk,bhk->bhv', s, -kk_t)
        s = (s * w_t[:, :, None, :] + sa[..., None] * (kk_t * a_t)[:, :, None, :]
             + v_t[..., None] * k_t[:, :, None, :])
        return s, jnp.einsum('bhvk,bhk->bhv', s, r_t)

    xs = tuple(jnp.moveaxis(z, 1, 0) for z in (r, decay, k, v, kk, a))
    s_final, out = lax.scan(step, s0, xs)
    return jnp.moveaxis(out, 0, 1), s_final


def rwkv7_mixer(p, state, v_first, w0, w_up, a0, a_up, g_up, k_k, k_a, r_k, ln_w, ln_b, v0, v_up):
    b, t, _ = p.shape
    f32 = jnp.float32
    pf = p.astype(f32)
    o1 = RWKV_WIDTH
    o2 = 2 * RWKV_WIDTH
    o3 = 3 * RWKV_WIDTH
    o4 = o3 + R_DECAY
    o5 = o4 + R_AAA
    o6 = o5 + R_GATE
    r, k, v = pf[..., :o1], pf[..., o1:o2], pf[..., o2:o3]
    xw, xa, xg = pf[..., o3:o4], pf[..., o4:o5], pf[..., o5:o6]
    w = -jax.nn.softplus(-(w0.astype(f32) + jnp.tanh(xw) @ w_up.astype(f32))) - 0.5
    decay = jnp.exp(-jnp.exp(w))
    a = jax.nn.sigmoid(a0.astype(f32) + xa @ a_up.astype(f32))
    g = jax.nn.sigmoid(xg) @ g_up.astype(f32)
    if v0 is None:
        v_first = v
    else:
        xv = pf[..., o6:]
        v = v + (v_first - v) * jax.nn.sigmoid(v0.astype(f32) + xv @ v_up.astype(f32))

    def heads(z):
        return z.reshape(b, t, RWKV_HEADS, RWKV_HEAD_DIM)

    kk = heads(k * k_k.astype(f32))
    kk = kk / jnp.maximum(jnp.sqrt(jnp.sum(kk * kk, axis=-1, keepdims=True)), 1e-12)
    k = k * (1.0 + (a - 1.0) * k_a.astype(f32))
    rh, kh, vh, ah, wh = heads(r), heads(k), heads(v), heads(a), heads(decay)
    out, s_new = rwkv7_scan(rh, wh, kh, vh, kk, ah, state.astype(f32))
    y = head_norm(out.reshape(b, t, RWKV_WIDTH), RWKV_HEADS, RWKV_LN_EPS) * ln_w.astype(f32) + ln_b.astype(f32)
    bonus = jnp.sum(rh * kh * r_k.astype(f32), axis=-1, keepdims=True) * vh
    y = (y + bonus.reshape(b, t, RWKV_WIDTH)) * g
    return y.astype(p.dtype), s_new, v_first


def trunk(x, pos, ret_state, rwkv_state, shift_state, wts):
    log_gamma = jnp.log1p(-jnp.exp2(-5.0 - jnp.arange(RET_HEADS, dtype=jnp.float32)))
    h = x
    v_first = None
    ret_out, rwkv_out, shift_out = [], [], []
    for l in range(DEPTH):
        xn = rms_norm(h, wts['g_attn'][l])
        if l == 0:
            w_mix, mu = wts['w_in'][0], wts['mu_shift'][0]
            v0, v_up = None, None
        else:
            w_mix = jnp.concatenate([wts['w_in'][l], wts['w_in_vres'][l - 1]], axis=-1)
            mu = jnp.concatenate([wts['mu_shift'][l], wts['mu_shift_vres'][l - 1]], axis=-1)
            v0, v_up = wts['rwkv_v0'][l - 1], wts['rwkv_v_up'][l - 1]
        proj = xn @ w_mix
        p_ret, p_rw = proj[..., :RET_COLS], proj[..., RET_COLS:]
        prev = shift_state[l].astype(xn.dtype) @ w_mix[:, RET_COLS:]
        p_prev = jnp.concatenate([prev[:, None, :], p_rw[:, :-1]], axis=1)
        p_rw = p_rw + (p_prev - p_rw) * mu
        y_ret, s_ret = retention_mixer(p_ret, ret_state[l], pos, wts['ret_ln_w'][l], log_gamma)
        y_rw, s_rw, v_first = rwkv7_mixer(
            p_rw, rwkv_state[l], v_first, wts['rwkv_w0'][l], wts['rwkv_w_up'][l], wts['rwkv_a0'][l],
            wts['rwkv_a_up'][l], wts['rwkv_g_up'][l], wts['rwkv_k_k'][l], wts['rwkv_k_a'][l],
            wts['rwkv_r_k'][l], wts['rwkv_ln_w'][l], wts['rwkv_ln_b'][l], v0, v_up)
        h = h + jnp.concatenate([y_ret, y_rw], axis=-1) @ wts['w_out'][l]
        hn = rms_norm(h, wts['g_ffn'][l])
        h = h + (jax.nn.silu(hn @ wts['w_gate'][l]) * (hn @ wts['w_up'][l])) @ wts['w_down'][l]
        ret_out.append(s_ret.astype(ret_state.dtype))
        rwkv_out.append(s_rw.astype(rwkv_state.dtype))
        shift_out.append(xn[:, -1, :].astype(shift_state.dtype))
    y = rms_norm(h, wts['g_final'])
    return y, jnp.stack(ret_out), jnp.stack(rwkv_out), jnp.stack(shift_out)


def setup_inputs(seed: int = 0) -> dict:
    key = jax.random.key(seed)
    ks = iter(jax.random.split(key, 40))
    f32 = jnp.float32

    def nrm(shape, scale):
        return jax.random.normal(next(ks), shape, f32) * scale

    def unif(shape, lo, hi):
        return jax.random.uniform(next(ks), shape, f32, minval=lo, maxval=hi)

    dm1 = DEPTH - 1
    return {
        "x_prompt": nrm((BATCH, SEQ, D_MODEL), 1.0),
        "x_sample": nrm((DEC_BATCH, DEC_SEQ, D_MODEL), 1.0),
        "state_ret": nrm((DEPTH, DEC_BATCH, RET_HEADS, RET_HEAD_DIM, RET_HEAD_DIM), 0.5),
        "state_rwkv": nrm((DEPTH, DEC_BATCH, RWKV_HEADS, RWKV_HEAD_DIM, RWKV_HEAD_DIM), 0.5),
        "state_shift": nrm((DEPTH, DEC_BATCH, D_MODEL), 1.0),
        "w_in": nrm((DEPTH, D_MODEL, IN_COLS), D_MODEL ** -0.5),
        "w_in_vres": nrm((dm1, D_MODEL, R_MV), D_MODEL ** -0.5),
        "mu_shift": unif((DEPTH, RWKV_COLS), 0.0, 1.0),
        "mu_shift_vres": unif((dm1, R_MV), 0.0, 1.0),
        "ret_ln_w": 1.0 + nrm((DEPTH, RET_WIDTH), 0.02),
        "rwkv_w0": unif((DEPTH, RWKV_WIDTH), -6.0, 0.0),
        "rwkv_w_up": nrm((DEPTH, R_DECAY, RWKV_WIDTH), 0.5 * R_DECAY ** -0.5),
        "rwkv_a0": nrm((DEPTH, RWKV_WIDTH), 0.1),
        "rwkv_a_up": nrm((DEPTH, R_AAA, RWKV_WIDTH), 0.5 * R_AAA ** -0.5),
        "rwkv_g_up": nrm((DEPTH, R_GATE, RWKV_WIDTH), R_GATE ** -0.5),
        "rwkv_v0": nrm((dm1, RWKV_WIDTH), 0.1),
        "rwkv_v_up": nrm((dm1, R_MV, RWKV_WIDTH), 0.5 * R_MV ** -0.5),
        "rwkv_k_k": 0.85 + nrm((DEPTH, RWKV_WIDTH), 0.02),
        "rwkv_k_a": 1.0 + nrm((DEPTH, RWKV_WIDTH), 0.02),
        "rwkv_r_k": nrm((DEPTH, RWKV_HEADS, RWKV_HEAD_DIM), 0.1),
        "rwkv_ln_w": 1.0 + nrm((DEPTH, RWKV_WIDTH), 0.02),
        "rwkv_ln_b": nrm((DEPTH, RWKV_WIDTH), 0.01),
        "w_out": nrm((DEPTH, D_MODEL, D_MODEL), D_MODEL ** -0.5),
        "g_attn": 1.0 + nrm((DEPTH, D_MODEL), 0.02),
        "g_ffn": 1.0 + nrm((DEPTH, D_MODEL), 0.02),
        "w_gate": nrm((DEPTH, D_MODEL, D_FF), D_MODEL ** -0.5),
        "w_up": nrm((DEPTH, D_MODEL, D_FF), D_MODEL ** -0.5),
        "w_down": nrm((DEPTH, D_FF, D_MODEL), D_FF ** -0.5),
        "g_final": 1.0 + nrm((D_MODEL,), 0.02),
    }


def reference(x_prompt, x_sample, state_ret, state_rwkv, state_shift, w_in, w_in_vres, mu_shift,
              mu_shift_vres, ret_ln_w, rwkv_w0, rwkv_w_up, rwkv_a0, rwkv_a_up, rwkv_g_up, rwkv_v0,
              rwkv_v_up, rwkv_k_k, rwkv_k_a, rwkv_r_k, rwkv_ln_w, rwkv_ln_b, w_out, g_attn, g_ffn,
              w_gate, w_up, w_down, g_final):
    wts = {
        'w_in': w_in, 'w_in_vres': w_in_vres, 'mu_shift': mu_shift, 'mu_shift_vres': mu_shift_vres,
        'ret_ln_w': ret_ln_w, 'rwkv_w0': rwkv_w0, 'rwkv_w_up': rwkv_w_up, 'rwkv_a0': rwkv_a0,
        'rwkv_a_up': rwkv_a_up, 'rwkv_g_up': rwkv_g_up, 'rwkv_v0': rwkv_v0, 'rwkv_v_up': rwkv_v_up,
        'rwkv_k_k': rwkv_k_k, 'rwkv_k_a': rwkv_k_a, 'rwkv_r_k': rwkv_r_k, 'rwkv_ln_w': rwkv_ln_w,
        'rwkv_ln_b': rwkv_ln_b, 'w_out': w_out, 'g_attn': g_attn, 'g_ffn': g_ffn,
        'w_gate': w_gate, 'w_up': w_up, 'w_down': w_down, 'g_final': g_final,
    }
    bp = x_prompt.shape[0]
    ret0 = jnp.zeros((DEPTH, bp) + state_ret.shape[2:], state_ret.dtype)
    rwkv0 = jnp.zeros((DEPTH, bp) + state_rwkv.shape[2:], state_rwkv.dtype)
    shift0 = jnp.zeros((DEPTH, bp, D_MODEL), state_shift.dtype)
    pos_prompt = jnp.arange(x_prompt.shape[1], dtype=jnp.float32)
    y_prompt, ret_p, rwkv_p, shift_p = trunk(x_prompt, pos_prompt, ret0, rwkv0, shift0, wts)
    pos_sample = PAST_LEN + jnp.arange(x_sample.shape[1], dtype=jnp.float32)
    y_sample, ret_s, rwkv_s, shift_s = trunk(x_sample, pos_sample, state_ret, state_rwkv, state_shift, wts)
    return (y_prompt, y_sample, ret_p, rwkv_p, shift_p, ret_s, rwkv_s, shift_s)
```

```python
import functools

import jax
import jax.numpy as jnp
from jax import lax
from jax.experimental import pallas as pl
from jax.experimental.pallas import tpu as pltpu

F32 = jnp.float32
BF16 = jnp.bfloat16

PAST_LEN = 16384
ROPE_BASE = 10000.0
RET_HEAD_DIM = 128
RET_CHUNK = 128
RWKV_HEAD_DIM = 64
RWKV_CHUNK = 64
NORM_EPS = 1e-6
RET_LN_EPS = 1e-5
RWKV_LN_EPS = 64e-5

LANES = 128
LORA_PAD = 512
VMEM_LIMIT_BYTES = 56 * 1024 * 1024
ROW_TILES = (832, 640, 512, 256, 128, 64, 32, 16)


def _params(*sem):
    return pltpu.CompilerParams(dimension_semantics=sem, vmem_limit_bytes=VMEM_LIMIT_BYTES)


def _pick(n, candidates):
    for c in candidates:
        if n % c == 0:
            return c
    raise ValueError(f"no tile in {candidates} divides {n}")


def _dot(a, b):
    return jnp.dot(a, b, preferred_element_type=F32)


def _dot_nt(a, b):
    return lax.dot_general(a, b, (((1,), (1,)), ((), ())), preferred_element_type=F32)


def _dot_tn(a, b):
    return lax.dot_general(a, b, (((0,), (0,)), ((), ())), preferred_element_type=F32)


def _bf(x):
    return x.astype(BF16)


def _rmsnorm_kernel(x_ref, g_ref, o_ref):
    x = x_ref[...]
    y = x * lax.rsqrt(jnp.mean(x * x, axis=-1, keepdims=True) + NORM_EPS)
    o_ref[...] = (y * g_ref[...]).astype(o_ref.dtype)


def _rmsnorm(x, g, out_dtype, *, row_start=0, rows=None, tm=None):
    m, d = x.shape
    rows = m if rows is None else rows
    if tm is None:
        tm = rows if rows <= 640 else _pick(rows, (640, 512, 256, 128, 64, 32, 16))
    assert row_start % tm == 0 and rows % tm == 0
    off = row_start // tm
    return pl.pallas_call(
        _rmsnorm_kernel,
        grid=(rows // tm,),
        in_specs=[pl.BlockSpec((tm, d), lambda i: (i + off, 0)), pl.BlockSpec((1, d), lambda i: (0, 0))],
        out_specs=pl.BlockSpec((tm, d), lambda i: (i, 0)),
        out_shape=jax.ShapeDtypeStruct((rows, d), out_dtype),
        compiler_params=_params("parallel"),
        name="rmsnorm",
    )(x, g.reshape(1, d))


def _mm_kernel(*refs, n_pairs, has_res):
    o_ref = refs[-1]
    acc = None
    for a_ref, w_ref in zip(refs[:n_pairs], refs[n_pairs:2 * n_pairs]):
        d = _dot(a_ref[...], w_ref[...])
        acc = d if acc is None else acc + d
    if has_res:
        acc = acc + refs[2 * n_pairs][...]
    o_ref[...] = acc.astype(o_ref.dtype)


def _matmul(pairs, n, out_dtype, *, res=None, col_start=0, tn=None, name="matmul"):
    m = pairs[0][0].shape[0]
    tm = _pick(m, ROW_TILES)
    if tn is None:
        tn = _pick(n, (512, 256, 128))
    assert col_start % tn == 0
    coff = col_start // tn
    in_specs, args = [], []
    for a, _, _ in pairs:
        in_specs.append(pl.BlockSpec((tm, a.shape[1]), lambda i, j: (i, 0)))
        args.append(a)
    for a, w, k_start in pairs:
        k = a.shape[1]
        assert k_start % k == 0
        in_specs.append(pl.BlockSpec((k, tn), lambda i, j, kb=k_start // k: (kb, j + coff)))
        args.append(w)
    if res is not None:
        in_specs.append(pl.BlockSpec((tm, tn), lambda i, j: (i, j)))
        args.append(res)
    return pl.pallas_call(
        functools.partial(_mm_kernel, n_pairs=len(pairs), has_res=res is not None),
        grid=(m // tm, n // tn),
        in_specs=in_specs,
        out_specs=pl.BlockSpec((tm, tn), lambda i, j: (i, j)),
        out_shape=jax.ShapeDtypeStruct((m, n), out_dtype),
        compiler_params=_params("parallel", "arbitrary"),
        name=name,
    )(*args)


def _swiglu_kernel(a_ref, wg_ref, wu_ref, o_ref):
    a = a_ref[...]
    g = _dot(a, wg_ref[...])
    u = _dot(a, wu_ref[...])
    o_ref[...] = (g * jax.nn.sigmoid(g) * u).astype(o_ref.dtype)


def _swiglu(a, wg, wu):
    m, k = a.shape
    n = wg.shape[1]
    tm = _pick(m, ROW_TILES)
    tn = _pick(n, (512, 256, 128))
    return pl.pallas_call(
        _swiglu_kernel,
        grid=(m // tm, n // tn),
        in_specs=[pl.BlockSpec((tm, k), lambda i, j: (i, 0)),
                  pl.BlockSpec((k, tn), lambda i, j: (0, j)),
                  pl.BlockSpec((k, tn), lambda i, j: (0, j))],
        out_specs=pl.BlockSpec((tm, tn), lambda i, j: (i, j)),
        out_shape=jax.ShapeDtypeStruct((m, n), BF16),
        compiler_params=_params("parallel", "arbitrary"),
        name="swiglu",
    )(a, wg, wu)


def _rope(x, cos, sin_signed):
    return x * cos + pltpu.roll(x, RET_HEAD_DIM // 2, 1) * sin_signed


def _head_norm(y, eps):
    mu = jnp.mean(y, axis=-1, keepdims=True)
    yc = y - mu
    var = jnp.mean(yc * yc, axis=-1, keepdims=True)
    return yc * lax.rsqrt(var + eps)


def _ret_prompt_kernel(q_ref, k_ref, v_ref, g_ref, cos_ref, sin_ref, lg_ref, lnw_ref, y_ref, s_ref, *, n_chunks):
    c, d = RET_CHUNK, RET_HEAD_DIM
    lg = lg_ref[0]
    row = lax.broadcasted_iota(jnp.int32, (c, c), 0)
    col = lax.broadcasted_iota(jnp.int32, (c, c), 1)
    diff = (row - col).astype(F32)
    decay_in = jnp.where(diff >= 0, jnp.exp(lg * jnp.maximum(diff, 0.0)), 0.0)
    ridx = lax.broadcasted_iota(jnp.int32, (c, d), 0).astype(F32)
    tail = jnp.exp((c - 1 - ridx) * lg)
    head = jnp.exp((ridx + 1) * lg)
    chunk_decay = jnp.exp(c * lg)
    lnw = lnw_ref[...]

    def body(n, s):
        rows = pl.ds(pl.multiple_of(n * c, c), c)
        cos, sin = cos_ref[rows, :], sin_ref[rows, :]
        q = _rope(q_ref[rows, :], cos, sin)
        k = _rope(k_ref[rows, :], cos, sin) * (d ** -0.5)
        vb = _bf(v_ref[rows, :])
        g = g_ref[rows, :]
        scores = _dot_nt(_bf(q), _bf(k)) * decay_in
        y = _dot(_bf(scores), vb) + _dot(_bf(q * head), _bf(s))
        s_new = s * chunk_decay + _dot_tn(_bf(k * tail), vb)
        yn = _head_norm(y, RET_LN_EPS) * lnw
        y_ref[rows, :] = (g * jax.nn.sigmoid(g) * yn).astype(y_ref.dtype)
        return s_new

    s_ref[0, 0] = lax.fori_loop(0, n_chunks, body, jnp.zeros((d, d), F32))


def _ret_prompt(proj, cos, sin, lg, ln_w, *, batch, seq, width):
    heads = width // RET_HEAD_DIM
    assert seq % RET_CHUNK == 0 and RET_CHUNK == LANES

    def col(which):
        return pl.BlockSpec((seq, RET_HEAD_DIM), lambda b, h: (b, which * heads + h))

    return pl.pallas_call(
        functools.partial(_ret_prompt_kernel, n_chunks=seq // RET_CHUNK),
        grid=(batch, heads),
        in_specs=[col(0), col(1), col(2), col(3),
                  pl.BlockSpec((seq, RET_HEAD_DIM), lambda b, h: (0, 0)),
                  pl.BlockSpec((seq, RET_HEAD_DIM), lambda b, h: (0, 0)),
                  pl.BlockSpec((1, 1, LANES), lambda b, h: (h, 0, 0)),
                  pl.BlockSpec((1, RET_HEAD_DIM), lambda b, h: (0, h))],
        out_specs=[pl.BlockSpec((seq, RET_HEAD_DIM), lambda b, h: (b, h)),
                   pl.BlockSpec((1, 1, RET_HEAD_DIM, RET_HEAD_DIM), lambda b, h: (b, h, 0, 0))],
        out_shape=[jax.ShapeDtypeStruct((batch * seq, width), BF16),
                   jax.ShapeDtypeStruct((batch, heads, RET_HEAD_DIM, RET_HEAD_DIM), F32)],
        compiler_params=_params("parallel", "parallel"),
        name="ret_prompt",
    )(proj, proj, proj, proj, cos, sin, lg, ln_w)


def _ret_sample_kernel(q_ref, k_ref, v_ref, g_ref, s_ref, cos_ref, sin_ref, lg_ref, lnw_ref, y_ref, o_ref):
    d = RET_HEAD_DIM
    cos, sin = cos_ref[...], sin_ref[...]
    q = _rope(q_ref[...], cos, sin)
    k = _rope(k_ref[...], cos, sin) * (d ** -0.5)
    v = v_ref[...]
    g = g_ref[...]
    gamma = jnp.exp(lg_ref[0])
    y = jnp.zeros_like(v)
    for i in range(d):
        new = s_ref[:, 0, i, :] * gamma + k[:, i:i + 1] * v
        o_ref[:, 0, i, :] = new
        y = y + q[:, i:i + 1] * new
    yn = _head_norm(y, RET_LN_EPS) * lnw_ref[...]
    y_ref[...] = (g * jax.nn.sigmoid(g) * yn).astype(y_ref.dtype)


def _ret_sample(proj, state, cos, sin, lg, ln_w, *, row_start, width):
    bs, heads = state.shape[0], state.shape[1]
    tt = _pick(bs, (16, 8))
    assert row_start % tt == 0
    roff = row_start // tt

    def col(which):
        return pl.BlockSpec((tt, RET_HEAD_DIM), lambda i, h: (i + roff, which * heads + h))

    st = pl.BlockSpec((tt, 1, RET_HEAD_DIM, RET_HEAD_DIM), lambda i, h: (i, h, 0, 0))
    return pl.pallas_call(
        _ret_sample_kernel,
        grid=(bs // tt, heads),
        in_specs=[col(0), col(1), col(2), col(3), st,
                  pl.BlockSpec((1, RET_HEAD_DIM), lambda i, h: (0, 0)),
                  pl.BlockSpec((1, RET_HEAD_DIM), lambda i, h: (0, 0)),
                  pl.BlockSpec((1, 1, LANES), lambda i, h: (h, 0, 0)),
                  pl.BlockSpec((1, RET_HEAD_DIM), lambda i, h: (0, h))],
        out_specs=[pl.BlockSpec((tt, RET_HEAD_DIM), lambda i, h: (i, h)), st],
        out_shape=[jax.ShapeDtypeStruct((bs, width), BF16), jax.ShapeDtypeStruct(state.shape, state.dtype)],
        compiler_params=_params("parallel", "parallel"),
        name="ret_sample",
    )(proj, proj, proj, proj, state, cos, sin, lg, ln_w)


def _softplus(x):
    return jnp.maximum(x, 0.0) + jnp.log1p(jnp.exp(-jnp.abs(x)))


def _rwkv_prep_kernel(*refs, seq_mode, has_vres, tiles_per_seq, width):
    it = iter(refs)
    cur = [next(it) for _ in range(4)]
    prev = [next(it) for _ in range(4)]
    mu_ref, w0_ref, wup_ref, a0_ref, aup_ref, gup_ref, kk_ref, ka_ref = (next(it) for _ in range(8))
    if has_vres:
        v0_ref, vup_ref, vfirst_ref = (next(it) for _ in range(3))
    r_out, lw_out, k_out, v_out, kk_out, a_out, g_out = (next(it) for _ in range(7))
    i = pl.program_id(0)

    def shifted(idx, mu):
        p = cur[idx][...]
        if seq_mode:
            carry = jnp.where(i % tiles_per_seq == 0, 0.0, prev[idx][7:8, :])
            rowi = lax.broadcasted_iota(jnp.int32, p.shape, 0)
            p_prev = jnp.where(rowi == 0, carry, pltpu.roll(p, 1, 0))
        else:
            p_prev = prev[idx][...]
        return p + (p_prev - p) * mu

    w = width
    r = shifted(0, mu_ref[:, 0:w])
    k = shifted(1, mu_ref[:, w:2 * w])
    v = shifted(2, mu_ref[:, 2 * w:3 * w])
    lo = shifted(3, mu_ref[:, 3 * w:])
    lob = _bf(lo)
    w_raw = -_softplus(-(w0_ref[...] + _dot(_bf(jnp.tanh(lo)), wup_ref[...]))) - 0.5
    a = jax.nn.sigmoid(a0_ref[...] + _dot(lob, aup_ref[...]))
    g = _dot(_bf(jax.nn.sigmoid(lo)), gup_ref[...])
    if has_vres:
        v = v + (vfirst_ref[...] - v) * jax.nn.sigmoid(v0_ref[...] + _dot(lob, vup_ref[...]))
    r_out[...] = r
    lw_out[...] = -jnp.exp(w_raw)
    k_out[...] = k * (1.0 + (a - 1.0) * ka_ref[...])
    v_out[...] = v
    kk_out[...] = k * kk_ref[...]
    a_out[...] = a
    g_out[...] = g


def _rwkv_prep(proj, prev_proj, lp, vfirst, *, row_start, rows, seq, ret_cols, width):
    seq_mode = prev_proj is None
    has_vres = vfirst is not None
    tp = _pick(seq, (256, 128)) if seq_mode else _pick(rows, (128, 64, 32, 16, 8))
    assert row_start % tp == 0 and ret_cols % width == 0 and (ret_cols + 3 * width) % LORA_PAD == 0
    roff = row_start // tp
    cb = ret_cols // width
    lb = (ret_cols + 3 * width) // LORA_PAD
    widths = (width, width, width, LORA_PAD)
    cur_blocks = (cb, cb + 1, cb + 2, lb)
    in_specs, args = [], []
    for wd, blk in zip(widths, cur_blocks):
        in_specs.append(pl.BlockSpec((tp, wd), lambda i, blk=blk: (i + roff, blk)))
        args.append(proj)
    if seq_mode:
        for wd, blk in zip(widths, cur_blocks):
            in_specs.append(pl.BlockSpec(
                (8, wd), lambda i, blk=blk: (jnp.maximum((i + roff) * (tp // 8) - 1, 0), blk)))
            args.append(proj)
    else:
        for wd, blk in zip(widths, (0, 1, 2, 3 * width // LORA_PAD)):
            in_specs.append(pl.BlockSpec((tp, wd), lambda i, blk=blk: (i, blk)))
            args.append(prev_proj)

    def full(x):
        return pl.BlockSpec(x.shape, lambda i: (0, 0))

    names = ["mu", "w0", "w_up", "a0", "a_up", "g_up", "k_k", "k_a"] + (["v0", "v_up"] if has_vres else [])
    for nm in names:
        in_specs.append(full(lp[nm]))
        args.append(lp[nm])
    if has_vres:
        in_specs.append(pl.BlockSpec((tp, width), lambda i: (i, 0)))
        args.append(vfirst)
    out_spec = pl.BlockSpec((tp, width), lambda i: (i, 0))
    return pl.pallas_call(
        functools.partial(_rwkv_prep_kernel, seq_mode=seq_mode, has_vres=has_vres,
                          tiles_per_seq=max(seq // tp, 1), width=width),
        grid=(rows // tp,),
        in_specs=in_specs,
        out_specs=[out_spec] * 7,
        out_shape=[jax.ShapeDtypeStruct((rows, width), F32)] * 7,
        compiler_params=_params("parallel"),
        name="rwkv_prep",
    )(*args)


def _rwkv_out(o, r, k2, v2, g, rk, lnw, lnb):
    y = _head_norm(o, RWKV_LN_EPS) * lnw + lnb
    bonus = jnp.sum(r * k2 * rk, axis=-1, keepdims=True) * v2
    return (y + bonus) * g


def _normalize_kk(kk):
    return kk / jnp.maximum(jnp.sqrt(jnp.sum(kk * kk, axis=-1, keepdims=True)), 1e-12)


def _cumsum_rows(x, tri):
    hi = _bf(x)
    r1 = x - hi.astype(F32)
    mid = _bf(r1)
    lo = _bf(r1 - mid.astype(F32))
    return _dot(tri, hi) + _dot(tri, mid) + _dot(tri, lo)


def _rwkv_chunk(r, lw, k2, v2, kk_raw, a, s, tri, strict, eye):
    c = r.shape[0]
    kk = _normalize_kk(kk_raw)
    b = kk * a
    cum = _cumsum_rows(lw, tri)
    tot = cum[c - 1:c, :]
    e_in = jnp.exp(-cum)
    e_out = jnp.exp(tot - cum)
    khat = _bf(kk * jnp.exp(cum - lw))
    rhat = _bf(r * jnp.exp(cum))
    btil = _bf(b * e_in)
    ktil = _bf(k2 * e_in)
    a_b = jnp.where(strict, _dot_nt(khat, btil), 0.0)
    a_k = jnp.where(strict, _dot_nt(khat, ktil), 0.0)
    a_rb = jnp.where(tri > 0, _dot_nt(rhat, btil), 0.0)
    a_rk = jnp.where(tri > 0, _dot_nt(rhat, ktil), 0.0)
    pw = -a_b
    tinv = eye + pw
    for _ in range(c.bit_length() - 2):
        pwb = _bf(pw)
        pw = _dot(pwb, pwb)
        tinv = tinv + _dot(_bf(tinv), _bf(pw))
    tinvb = _bf(tinv)
    v2b = _bf(v2)
    sb = _bf(s)
    wt = _dot(tinvb, khat)
    u0 = _dot(tinvb, _bf(_dot(_bf(a_k), v2b)))
    u = -(_dot_nt(_bf(wt), sb) + u0)
    ub = _bf(u)
    o = _dot_nt(rhat, sb) + _dot(_bf(a_rb), ub) + _dot(_bf(a_rk), v2b)
    s_new = s * jnp.exp(tot) + _dot_tn(ub, _bf(b * e_out)) + _dot_tn(v2b, _bf(k2 * e_out))
    return o, s_new


def _rwkv_prompt_kernel(r_ref, lw_ref, k_ref, v_ref, kk_ref, a_ref, g_ref, rk_ref, lnw_ref, lnb_ref,
                        y_ref, s_ref, *, n_chunks):
    c, n = RWKV_CHUNK, RWKV_HEAD_DIM
    row = lax.broadcasted_iota(jnp.int32, (c, c), 0)
    col = lax.broadcasted_iota(jnp.int32, (c, c), 1)
    tri = jnp.where(row >= col, 1.0, 0.0).astype(BF16)
    strict = row > col
    eye = jnp.where(row == col, 1.0, 0.0).astype(F32)
    heads = r_ref.shape[1] // n

    def body(ci, states):
        rows = pl.ds(pl.multiple_of(ci * c, c), c)
        new_states, ys = [], []
        for hh in range(heads):
            sl = slice(hh * n, (hh + 1) * n)
            r, k2, v2 = r_ref[rows, sl], k_ref[rows, sl], v_ref[rows, sl]
            o, s_new = _rwkv_chunk(r, lw_ref[rows, sl], k2, v2, kk_ref[rows, sl], a_ref[rows, sl],
                                   states[hh], tri, strict, eye)
            ys.append(_rwkv_out(o, r, k2, v2, g_ref[rows, sl], rk_ref[:, sl], lnw_ref[:, sl], lnb_ref[:, sl]))
            new_states.append(s_new)
        y_ref[rows, :] = jnp.concatenate(ys, axis=1).astype(y_ref.dtype)
        return tuple(new_states)

    final = lax.fori_loop(0, n_chunks, body, tuple(jnp.zeros((n, n), F32) for _ in range(heads)))
    for hh in range(heads):
        s_ref[0, hh] = final[hh]


def _rwkv_prompt(feats, rk, lnw, lnb, *, batch, seq, width):
    heads = width // RWKV_HEAD_DIM
    hp = LANES // RWKV_HEAD_DIM
    assert seq % RWKV_CHUNK == 0 and heads % hp == 0
    tok = pl.BlockSpec((seq, LANES), lambda b, h: (b, h))
    par = pl.BlockSpec((1, LANES), lambda b, h: (0, h))
    return pl.pallas_call(
        functools.partial(_rwkv_prompt_kernel, n_chunks=seq // RWKV_CHUNK),
        grid=(batch, heads // hp),
        in_specs=[tok] * 7 + [par] * 3,
        out_specs=[tok, pl.BlockSpec((1, hp, RWKV_HEAD_DIM, RWKV_HEAD_DIM), lambda b, h: (b, h, 0, 0))],
        out_shape=[jax.ShapeDtypeStruct((batch * seq, width), BF16),
                   jax.ShapeDtypeStruct((batch, heads, RWKV_HEAD_DIM, RWKV_HEAD_DIM), F32)],
        compiler_params=_params("parallel", "parallel"),
        name="rwkv_prompt",
    )(*feats, rk, lnw, lnb)


def _rwkv_sample_kernel(r_ref, lw_ref, k_ref, v_ref, kk_ref, a_ref, g_ref, s_ref, rk_ref, lnw_ref, lnb_ref,
                        y_ref, o_ref):
    n = RWKV_HEAD_DIM
    heads = r_ref.shape[1] // n
    ys = []
    for hh in range(heads):
        sl = slice(hh * n, (hh + 1) * n)
        r, k2, v2 = r_ref[:, sl], k_ref[:, sl], v_ref[:, sl]
        kk = _normalize_kk(kk_ref[:, sl])
        b = kk * a_ref[:, sl]
        w = jnp.exp(lw_ref[:, sl])
        lane = lax.broadcasted_iota(jnp.int32, r.shape, 1)
        out = jnp.zeros_like(r)
        for i in range(n):
            srow = s_ref[:, hh, i, :]
            sa = -jnp.sum(srow * kk, axis=-1, keepdims=True)
            new = srow * w + sa * b + v2[:, i:i + 1] * k2
            o_ref[:, hh, i, :] = new
            out = jnp.where(lane == i, jnp.sum(new * r, axis=-1, keepdims=True), out)
        ys.append(_rwkv_out(out, r, k2, v2, g_ref[:, sl], rk_ref[:, sl], lnw_ref[:, sl], lnb_ref[:, sl]))
    y_ref[...] = jnp.concatenate(ys, axis=1).astype(y_ref.dtype)


def _rwkv_sample(feats, state, rk, lnw, lnb, *, width):
    bs, heads = state.shape[0], state.shape[1]
    hp = LANES // RWKV_HEAD_DIM
    tt = _pick(bs, (16, 8))
    tok = pl.BlockSpec((tt, LANES), lambda i, h: (i, h))
    par = pl.BlockSpec((1, LANES), lambda i, h: (0, h))
    st = pl.BlockSpec((tt, hp, RWKV_HEAD_DIM, RWKV_HEAD_DIM), lambda i, h: (i, h, 0, 0))
    return pl.pallas_call(
        _rwkv_sample_kernel,
        grid=(bs // tt, heads // hp),
        in_specs=[tok] * 7 + [st] + [par] * 3,
        out_specs=[tok, st],
        out_shape=[jax.ShapeDtypeStruct((bs, width), BF16), jax.ShapeDtypeStruct(state.shape, state.dtype)],
        compiler_params=_params("parallel", "parallel"),
        name="rwkv_sample",
    )(*feats, state, rk, lnw, lnb)


def _rope_tables(pos):
    half = RET_HEAD_DIM // 2
    inv_freq = ROPE_BASE ** (-jnp.arange(half, dtype=F32) / half)
    ang = pos[:, None] * inv_freq[None, :]
    cos, sin = jnp.cos(ang), jnp.sin(ang)
    return jnp.concatenate([cos, cos], axis=-1), jnp.concatenate([-sin, sin], axis=-1)


def _pad_rows(w, start, total):
    return jnp.zeros((total, w.shape[1]), BF16).at[start:start + w.shape[0]].set(_bf(w))


def kernel(x_prompt, x_sample, state_ret, state_rwkv, state_shift, w_in, w_in_vres, mu_shift, mu_shift_vres, ret_ln_w, rwkv_w0, rwkv_w_up, rwkv_a0, rwkv_a_up, rwkv_g_up, rwkv_v0, rwkv_v_up, rwkv_k_k, rwkv_k_a, rwkv_r_k, rwkv_ln_w, rwkv_ln_b, w_out, g_attn, g_ffn, w_gate, w_up, w_down, g_final):
    batch, seq, d = x_prompt.shape
    bs = x_sample.shape[0]
    assert x_sample.shape[1] == 1
    depth = w_in.shape[0]
    mp = batch * seq
    ret_w = d // 2
    rw_w = d - ret_w
    ret_cols = 4 * ret_w
    main_cols = ret_cols + 3 * rw_w
    r_decay, r_aaa, r_gate = rwkv_w_up.shape[1], rwkv_a_up.shape[1], rwkv_g_up.shape[1]
    r_mv = rwkv_v_up.shape[1]
    o_a, o_g = r_decay, r_decay + r_aaa
    o_v = o_g + r_gate
    assert o_v + r_mv <= LORA_PAD and w_in.shape[2] == main_cols + o_v
    ret_heads = ret_w // RET_HEAD_DIM

    log_gamma = jnp.log1p(-jnp.exp2(-5.0 - jnp.arange(ret_heads, dtype=F32)))
    lg = jnp.broadcast_to(log_gamma[:, None, None], (ret_heads, 1, LANES))
    cos_p, sin_p = _rope_tables(jnp.arange(seq, dtype=F32))
    cos_s, sin_s = _rope_tables(PAST_LEN + jnp.arange(1, dtype=F32))

    h = jnp.concatenate([x_prompt.reshape(mp, d), x_sample.reshape(bs, d)], axis=0)
    vfirst_p = vfirst_s = None
    ret_p, rwkv_p, shift_p, ret_s, rwkv_s, shift_s = [], [], [], [], [], []
    for l in range(depth):
        lora_cols = [w_in[l][:, main_cols:]] + ([w_in_vres[l - 1]] if l > 0 else [])
        lora_mu = [mu_shift[l][3 * rw_w:]] + ([mu_shift_vres[l - 1]] if l > 0 else [])
        n_lora = sum(x.shape[1] for x in lora_cols)
        w_mix = _bf(jnp.concatenate([w_in[l][:, :main_cols]] + lora_cols
                                    + [jnp.zeros((d, LORA_PAD - n_lora), F32)], axis=1))
        mu = jnp.concatenate([mu_shift[l][:3 * rw_w]] + lora_mu + [jnp.zeros((LORA_PAD - n_lora,), F32)])
        lp = {
            "mu": mu.reshape(1, -1),
            "w0": rwkv_w0[l].reshape(1, -1), "w_up": _pad_rows(rwkv_w_up[l], 0, LORA_PAD),
            "a0": rwkv_a0[l].reshape(1, -1), "a_up": _pad_rows(rwkv_a_up[l], o_a, LORA_PAD),
            "g_up": _pad_rows(rwkv_g_up[l], o_g, LORA_PAD),
            "k_k": rwkv_k_k[l].reshape(1, -1), "k_a": rwkv_k_a[l].reshape(1, -1),
        }
        if l > 0:
            lp["v0"] = rwkv_v0[l - 1].reshape(1, -1)
            lp["v_up"] = _pad_rows(rwkv_v_up[l - 1], o_v, LORA_PAD)
        rk = rwkv_r_k[l].reshape(1, -1)
        lnw, lnb = rwkv_ln_w[l].reshape(1, -1), rwkv_ln_b[l].reshape(1, -1)
        ret_lnw = ret_ln_w[l].reshape(1, -1)

        xn = _rmsnorm(h, g_attn[l], BF16)
        proj = _matmul([(xn, w_mix, 0)], main_cols + LORA_PAD, F32, name="in_proj")
        h_last = jnp.concatenate([h[:mp].reshape(batch, seq, d)[:, -1], h[mp:]], axis=0)
        xn_last = _rmsnorm(h_last, g_attn[l], F32)
        shift_p.append(xn_last[:batch])
        shift_s.append(xn_last[batch:])
        prev_s = _matmul([(_bf(state_shift[l]), w_mix, 0)], 3 * rw_w + LORA_PAD, F32,
                         col_start=ret_cols, name="prev_proj")

        feats_p = _rwkv_prep(proj, None, lp, vfirst_p, row_start=0, rows=mp, seq=seq,
                             ret_cols=ret_cols, width=rw_w)
        feats_s = _rwkv_prep(proj, prev_s, lp, vfirst_s, row_start=mp, rows=bs, seq=1,
                             ret_cols=ret_cols, width=rw_w)
        if l == 0:
            vfirst_p, vfirst_s = feats_p[3], feats_s[3]

        y_ret_p, s_ret_p = _ret_prompt(proj, cos_p, sin_p, lg, ret_lnw, batch=batch, seq=seq, width=ret_w)
        y_ret_s, s_ret_s = _ret_sample(proj, state_ret[l], cos_s, sin_s, lg, ret_lnw, row_start=mp, width=ret_w)
        y_rw_p, s_rw_p = _rwkv_prompt(feats_p, rk, lnw, lnb, batch=batch, seq=seq, width=rw_w)
        y_rw_s, s_rw_s = _rwkv_sample(feats_s, state_rwkv[l], rk, lnw, lnb, width=rw_w)
        ret_p.append(s_ret_p)
        ret_s.append(s_ret_s)
        rwkv_p.append(s_rw_p)
        rwkv_s.append(s_rw_s)

        y_ret = jnp.concatenate([y_ret_p, y_ret_s], axis=0)
        y_rw = jnp.concatenate([y_rw_p, y_rw_s], axis=0)
        w_o = _bf(w_out[l])
        h = _matmul([(y_ret, w_o, 0), (y_rw, w_o, ret_w)], d, F32, res=h, name="out_proj")
        hn = _rmsnorm(h, g_ffn[l], BF16)
        act = _swiglu(hn, _bf(w_gate[l]), _bf(w_up[l]))
        h = _matmul([(act, _bf(w_down[l]), 0)], d, F32, res=h, name="ffn_down")

    y_p = _rmsnorm(h, g_final, F32, row_start=0, rows=mp).reshape(batch, seq, d)
    y_s = _rmsnorm(h, g_final, F32, row_start=mp, rows=bs, tm=_pick(bs, (128, 64, 32, 16, 8))).reshape(bs, 1, d)
    return (y_p, y_s, jnp.stack(ret_p), jnp.stack(rwkv_p), jnp.stack(shift_p),
            jnp.stack(ret_s), jnp.stack(rwkv_s), jnp.stack(shift_s))
```

```python
import functools

import jax
import jax.numpy as jnp
from jax import lax
from jax.experimental import pallas as pl
from jax.experimental.pallas import tpu as pltpu

F32 = jnp.float32
BF16 = jnp.bfloat16

PAST_LEN = 16384
ROPE_BASE = 10000.0
RET_HEAD_DIM = 128
RET_CHUNK = 128
RWKV_HEAD_DIM = 64
RWKV_CHUNK = 64
NORM_EPS = 1e-6
RET_LN_EPS = 1e-5
RWKV_LN_EPS = 64e-5

LANES = 128
SUBLANES = 8
LORA_PAD = 512
VMEM_LIMIT_BYTES = 56 * 1024 * 1024
ROW_TILES = (832, 640, 512, 256, 128, 64, 32, 16)
STEP_TOKENS = 8


def _params(*sem):
    return pltpu.CompilerParams(dimension_semantics=sem, vmem_limit_bytes=VMEM_LIMIT_BYTES)


def _pick(n, candidates):
    for c in candidates:
        if n % c == 0:
            return c
    raise ValueError(f"no tile in {candidates} divides {n}")


def _dot(a, b):
    return jnp.dot(a, b, preferred_element_type=F32)


def _dot_nt(a, b):
    return lax.dot_general(a, b, (((1,), (1,)), ((), ())), preferred_element_type=F32)


def _dot_tn(a, b):
    return lax.dot_general(a, b, (((0,), (0,)), ((), ())), preferred_element_type=F32)


def _bf(x):
    return x.astype(BF16)


def _split2(x):
    hi = _bf(x)
    return hi, _bf(x - hi.astype(F32))


def _rmsnorm_kernel(x_ref, g_ref, o_ref):
    x = x_ref[...]
    y = x * lax.rsqrt(jnp.mean(x * x, axis=-1, keepdims=True) + NORM_EPS)
    o_ref[...] = (y * g_ref[...]).astype(o_ref.dtype)


def _rmsnorm(x, g, out_dtype, *, row_start=0, rows=None, tm=None):
    m, d = x.shape
    rows = m if rows is None else rows
    if tm is None:
        tm = rows if rows <= 640 else _pick(rows, (640, 512, 256, 128, 64, 32, 16))
    assert row_start % tm == 0 and rows % tm == 0
    off = row_start // tm
    return pl.pallas_call(
        _rmsnorm_kernel,
        grid=(rows // tm,),
        in_specs=[pl.BlockSpec((tm, d), lambda i: (i + off, 0)), pl.BlockSpec((1, d), lambda i: (0, 0))],
        out_specs=pl.BlockSpec((tm, d), lambda i: (i, 0)),
        out_shape=jax.ShapeDtypeStruct((rows, d), out_dtype),
        compiler_params=_params("parallel"),
        name="rmsnorm",
    )(x, g.reshape(1, d))


def _mm_kernel(*refs, n_pairs, has_res):
    o_ref = refs[-1]
    acc = None
    for a_ref, w_ref in zip(refs[:n_pairs], refs[n_pairs:2 * n_pairs]):
        d = _dot(a_ref[...], w_ref[...])
        acc = d if acc is None else acc + d
    if has_res:
        acc = acc + refs[2 * n_pairs][...]
    o_ref[...] = acc.astype(o_ref.dtype)


def _matmul(pairs, n, out_dtype, *, res=None, col_start=0, tn=None, name="matmul"):
    m = pairs[0][0].shape[0]
    tm = _pick(m, ROW_TILES)
    if tn is None:
        tn = _pick(n, (512, 256, 128))
    assert col_start % tn == 0
    coff = col_start // tn
    in_specs, args = [], []
    for a, _, _ in pairs:
        in_specs.append(pl.BlockSpec((tm, a.shape[1]), lambda i, j: (i, 0)))
        args.append(a)
    for a, w, k_start in pairs:
        k = a.shape[1]
        assert k_start % k == 0
        in_specs.append(pl.BlockSpec((k, tn), lambda i, j, kb=k_start // k: (kb, j + coff)))
        args.append(w)
    if res is not None:
        in_specs.append(pl.BlockSpec((tm, tn), lambda i, j: (i, j)))
        args.append(res)
    return pl.pallas_call(
        functools.partial(_mm_kernel, n_pairs=len(pairs), has_res=res is not None),
        grid=(m // tm, n // tn),
        in_specs=in_specs,
        out_specs=pl.BlockSpec((tm, tn), lambda i, j: (i, j)),
        out_shape=jax.ShapeDtypeStruct((m, n), out_dtype),
        compiler_params=_params("parallel", "arbitrary"),
        name=name,
    )(*args)


def _swiglu_kernel(a_ref, wg_ref, wu_ref, o_ref):
    a = a_ref[...]
    g = _dot(a, wg_ref[...])
    u = _dot(a, wu_ref[...])
    o_ref[...] = (g * jax.nn.sigmoid(g) * u).astype(o_ref.dtype)


def _swiglu(a, wg, wu):
    m, k = a.shape
    n = wg.shape[1]
    tm = _pick(m, ROW_TILES)
    tn = _pick(n, (512, 256, 128))
    return pl.pallas_call(
        _swiglu_kernel,
        grid=(m // tm, n // tn),
        in_specs=[pl.BlockSpec((tm, k), lambda i, j: (i, 0)),
                  pl.BlockSpec((k, tn), lambda i, j: (0, j)),
                  pl.BlockSpec((k, tn), lambda i, j: (0, j))],
        out_specs=pl.BlockSpec((tm, tn), lambda i, j: (i, j)),
        out_shape=jax.ShapeDtypeStruct((m, n), BF16),
        compiler_params=_params("parallel", "arbitrary"),
        name="swiglu",
    )(a, wg, wu)


def _rope(x, cos, sin_signed):
    return x * cos + pltpu.roll(x, RET_HEAD_DIM // 2, 1) * sin_signed


def _head_norm(y, eps):
    mu = jnp.mean(y, axis=-1, keepdims=True)
    yc = y - mu
    var = jnp.mean(yc * yc, axis=-1, keepdims=True)
    return yc * lax.rsqrt(var + eps)


def _ret_prompt_kernel(q_ref, k_ref, v_ref, g_ref, cos_ref, sin_ref, lg_ref, lnw_ref, y_ref, s_ref, *, n_chunks):
    c, d = RET_CHUNK, RET_HEAD_DIM
    lg = lg_ref[0]
    row = lax.broadcasted_iota(jnp.int32, (c, c), 0)
    col = lax.broadcasted_iota(jnp.int32, (c, c), 1)
    diff = (row - col).astype(F32)
    decay_in = jnp.where(diff >= 0, jnp.exp(lg * jnp.maximum(diff, 0.0)), 0.0)
    ridx = lax.broadcasted_iota(jnp.int32, (c, d), 0).astype(F32)
    tail = jnp.exp((c - 1 - ridx) * lg)
    head = jnp.exp((ridx + 1) * lg)
    chunk_decay = jnp.exp(c * lg)
    lnw = lnw_ref[...]

    def body(n, s):
        rows = pl.ds(pl.multiple_of(n * c, c), c)
        cos, sin = cos_ref[rows, :], sin_ref[rows, :]
        q = _rope(q_ref[rows, :], cos, sin)
        k = _rope(k_ref[rows, :], cos, sin) * (d ** -0.5)
        vb = _bf(v_ref[rows, :])
        g = g_ref[rows, :]
        scores = _dot_nt(_bf(q), _bf(k)) * decay_in
        y = _dot(_bf(scores), vb) + _dot(_bf(q * head), _bf(s))
        s_new = s * chunk_decay + _dot_tn(_bf(k * tail), vb)
        yn = _head_norm(y, RET_LN_EPS) * lnw
        y_ref[rows, :] = (g * jax.nn.sigmoid(g) * yn).astype(y_ref.dtype)
        return s_new

    s_ref[0, 0] = lax.fori_loop(0, n_chunks, body, jnp.zeros((d, d), F32))


def _ret_prompt(proj, cos, sin, lg, ln_w, *, batch, seq, width):
    heads = width // RET_HEAD_DIM
    assert seq % RET_CHUNK == 0 and RET_CHUNK == LANES

    def col(which):
        return pl.BlockSpec((seq, RET_HEAD_DIM), lambda b, h: (b, which * heads + h))

    return pl.pallas_call(
        functools.partial(_ret_prompt_kernel, n_chunks=seq // RET_CHUNK),
        grid=(batch, heads),
        in_specs=[col(0), col(1), col(2), col(3),
                  pl.BlockSpec((seq, RET_HEAD_DIM), lambda b, h: (0, 0)),
                  pl.BlockSpec((seq, RET_HEAD_DIM), lambda b, h: (0, 0)),
                  pl.BlockSpec((1, 1, LANES), lambda b, h: (h, 0, 0)),
                  pl.BlockSpec((1, RET_HEAD_DIM), lambda b, h: (0, h))],
        out_specs=[pl.BlockSpec((seq, RET_HEAD_DIM), lambda b, h: (b, h)),
                   pl.BlockSpec((1, 1, RET_HEAD_DIM, RET_HEAD_DIM), lambda b, h: (b, h, 0, 0))],
        out_shape=[jax.ShapeDtypeStruct((batch * seq, width), BF16),
                   jax.ShapeDtypeStruct((batch, heads, RET_HEAD_DIM, RET_HEAD_DIM), F32)],
        compiler_params=_params("parallel", "parallel"),
        name="ret_prompt",
    )(proj, proj, proj, proj, cos, sin, lg, ln_w)


def _ret_sample_kernel(q_ref, k_ref, v_ref, g_ref, s_ref, cos_ref, sin_ref, lg_ref, lnw_ref, y_ref, o_ref,
                       q_scr, kp_scr, v_scr, y_scr):
    d = RET_HEAD_DIM
    tt, width = q_ref.shape
    heads = width // d
    cos, sin = cos_ref[...], sin_ref[...]
    v_scr[:, 0, :] = v_ref[...]
    for h in range(heads):
        ln = slice(h * d, (h + 1) * d)
        q_scr[:, 0, ln] = _rope(q_ref[:, ln], cos, sin)
        k = _rope(k_ref[:, ln], cos, sin) * (d ** -0.5)
        k_hi = _bf(k).astype(F32)
        kp_scr[:, 0, 2 * h * d:(2 * h + 1) * d] = k_hi
        kp_scr[:, 0, (2 * h + 1) * d:(2 * h + 2) * d] = k - k_hi
    row = lax.broadcasted_iota(jnp.int32, (d, 2 * d), 0)
    col = lax.broadcasted_iota(jnp.int32, (d, 2 * d), 1)
    eye2 = jnp.where((col == row) | (col == row + d), 1.0, 0.0).astype(BF16)

    def body(t, carry):
        for h in range(heads):
            ln = slice(h * d, (h + 1) * d)
            kp = _bf(jnp.broadcast_to(kp_scr[t, :, 2 * h * d:(2 * h + 2) * d], (d, 2 * d)))
            k_col = _dot_nt(eye2, kp)
            s_new = s_ref[t, h] * jnp.exp(lg_ref[h]) + k_col * v_scr[t, :, ln]
            o_ref[t, h] = s_new
            qb = _bf(jnp.broadcast_to(q_scr[t, :, ln], (SUBLANES, d)))
            y_scr[t, :, ln] = _dot(qb, _bf(s_new))[0:1]
        return carry

    lax.fori_loop(0, tt, body, 0)
    for h in range(heads):
        ln = slice(h * d, (h + 1) * d)
        g = g_ref[:, ln]
        yn = _head_norm(y_scr[:, 0, ln], RET_LN_EPS) * lnw_ref[:, ln]
        y_ref[:, ln] = (g * jax.nn.sigmoid(g) * yn).astype(y_ref.dtype)


def _ret_sample(proj, state, cos, sin, lg, ln_w, *, row_start, width):
    bs, heads = state.shape[0], state.shape[1]
    tt = STEP_TOKENS
    assert row_start % tt == 0 and bs % tt == 0
    roff = row_start // tt

    def col(which):
        return pl.BlockSpec((tt, width), lambda i: (i + roff, which))

    st = pl.BlockSpec((tt, heads, RET_HEAD_DIM, RET_HEAD_DIM), lambda i: (i, 0, 0, 0))
    return pl.pallas_call(
        _ret_sample_kernel,
        grid=(bs // tt,),
        in_specs=[col(0), col(1), col(2), col(3), st,
                  pl.BlockSpec((1, RET_HEAD_DIM), lambda i: (0, 0)),
                  pl.BlockSpec((1, RET_HEAD_DIM), lambda i: (0, 0)),
                  pl.BlockSpec((heads, 1, LANES), lambda i: (0, 0, 0)),
                  pl.BlockSpec((1, width), lambda i: (0, 0))],
        out_specs=[pl.BlockSpec((tt, width), lambda i: (i, 0)), st],
        out_shape=[jax.ShapeDtypeStruct((bs, width), BF16), jax.ShapeDtypeStruct(state.shape, state.dtype)],
        scratch_shapes=[pltpu.VMEM((tt, 1, width), F32), pltpu.VMEM((tt, 1, 2 * width), F32),
                        pltpu.VMEM((tt, 1, width), F32), pltpu.VMEM((tt, 1, width), F32)],
        compiler_params=_params("parallel"),
        name="ret_sample",
    )(proj, proj, proj, proj, state, cos, sin, lg, ln_w)


def _softplus(x):
    return jnp.maximum(x, 0.0) + jnp.log1p(jnp.exp(-jnp.abs(x)))


def _group_matrix():
    r = lax.broadcasted_iota(jnp.int32, (LANES, LANES), 0) // RWKV_HEAD_DIM
    c = lax.broadcasted_iota(jnp.int32, (LANES, LANES), 1) // RWKV_HEAD_DIM
    return jnp.where(r == c, 1.0, 0.0).astype(BF16)


def _group_sum(x, gmat):
    outs = []
    for j in range(x.shape[1] // LANES):
        hi, lo = _split2(x[:, j * LANES:(j + 1) * LANES])
        outs.append(_dot(hi, gmat) + _dot(lo, gmat))
    return outs[0] if len(outs) == 1 else jnp.concatenate(outs, axis=1)


def _cumsum_rows(x, tri):
    hi = _bf(x)
    r1 = x - hi.astype(F32)
    mid = _bf(r1)
    lo = _bf(r1 - mid.astype(F32))
    return _dot(tri, hi) + _dot(tri, mid) + _dot(tri, lo)


def _rwkv_prep_kernel(*refs, seq_mode, has_vres, tiles_per_seq, width):
    it = iter(refs)
    cur = [next(it) for _ in range(4)]
    prev = [next(it) for _ in range(4)]
    mu_ref, w0_ref, wup_ref, a0_ref, aup_ref, gup_ref, kk_ref, ka_ref, rk_ref = (next(it) for _ in range(9))
    if has_vres:
        v0_ref, vup_ref, vfirst_ref = (next(it) for _ in range(3))
    outs = list(it)
    i = pl.program_id(0)

    def shifted(idx, mu):
        p = cur[idx][...]
        if seq_mode:
            carry = jnp.where(i % tiles_per_seq == 0, 0.0, prev[idx][SUBLANES - 1:SUBLANES, :])
            rowi = lax.broadcasted_iota(jnp.int32, p.shape, 0)
            p_prev = jnp.where(rowi == 0, carry, pltpu.roll(p, 1, 0))
        else:
            p_prev = prev[idx][...]
        return p + (p_prev - p) * mu

    w = width
    r = shifted(0, mu_ref[:, 0:w])
    k = shifted(1, mu_ref[:, w:2 * w])
    v = shifted(2, mu_ref[:, 2 * w:3 * w])
    lo = shifted(3, mu_ref[:, 3 * w:])
    lob = _bf(lo)
    w_raw = -_softplus(-(w0_ref[...] + _dot(_bf(jnp.tanh(lo)), wup_ref[...]))) - 0.5
    lw = -jnp.exp(w_raw)
    a = jax.nn.sigmoid(a0_ref[...] + _dot(lob, aup_ref[...]))
    g = _dot(_bf(jax.nn.sigmoid(lo)), gup_ref[...])
    if has_vres:
        v = v + (vfirst_ref[...] - v) * jax.nn.sigmoid(v0_ref[...] + _dot(lob, vup_ref[...]))
    k2 = k * (1.0 + (a - 1.0) * ka_ref[...])
    kk = k * kk_ref[...]
    gmat = _group_matrix()
    kk = kk / jnp.maximum(jnp.sqrt(_group_sum(kk * kk, gmat)), 1e-12)
    b = kk * a
    if not seq_mode:
        for o_ref, val in zip(outs, (r, lw, k2, v, kk, b, g)):
            o_ref[...] = val
        return

    tp, c = r.shape[0], RWKV_CHUNK
    row = lax.broadcasted_iota(jnp.int32, (tp, tp), 0)
    col = lax.broadcasted_iota(jnp.int32, (tp, tp), 1)
    tri = jnp.where(row >= col, jnp.where(row // c == col // c, 1.0, 0.0), 0.0).astype(BF16)
    cum = _cumsum_rows(lw, tri)
    last = [cum[j * c + c - 1:j * c + c, :] for j in range(tp // c)]
    tot = jnp.concatenate([jnp.broadcast_to(x, (c, w)) for x in last], axis=0)
    e_in = jnp.exp(-cum)
    e_out = jnp.exp(tot - cum)
    bonus = _group_sum(r * k2 * rk_ref[...], gmat) * v
    vals = (kk * jnp.exp(cum - lw), r * jnp.exp(cum), b * e_in, k2 * e_in, b * e_out, k2 * e_out, v, bonus, g)
    for o_ref, val in zip(outs, vals):
        o_ref[...] = val.astype(o_ref.dtype)
    outs[9][...] = jnp.concatenate([jnp.broadcast_to(jnp.exp(x), (SUBLANES, w)) for x in last], axis=0)
    if not has_vres:
        outs[10][...] = v


def _rwkv_prep(proj, prev_proj, lp, vfirst, *, row_start, rows, seq, ret_cols, width):
    seq_mode = prev_proj is None
    has_vres = vfirst is not None
    tp = _pick(seq, (256, 128, 64)) if seq_mode else _pick(rows, (128, 64, 32, 16, 8))
    assert row_start % tp == 0 and ret_cols % width == 0 and (ret_cols + 3 * width) % LORA_PAD == 0
    roff = row_start // tp
    cb = ret_cols // width
    lb = (ret_cols + 3 * width) // LORA_PAD
    widths = (width, width, width, LORA_PAD)
    cur_blocks = (cb, cb + 1, cb + 2, lb)
    in_specs, args = [], []
    for wd, blk in zip(widths, cur_blocks):
        in_specs.append(pl.BlockSpec((tp, wd), lambda i, blk=blk: (i + roff, blk)))
        args.append(proj)
    if seq_mode:
        for wd, blk in zip(widths, cur_blocks):
            in_specs.append(pl.BlockSpec(
                (SUBLANES, wd), lambda i, blk=blk: (jnp.maximum((i + roff) * (tp // SUBLANES) - 1, 0), blk)))
            args.append(proj)
    else:
        for wd, blk in zip(widths, (0, 1, 2, 3 * width // LORA_PAD)):
            in_specs.append(pl.BlockSpec((tp, wd), lambda i, blk=blk: (i, blk)))
            args.append(prev_proj)

    def full(x):
        return pl.BlockSpec(x.shape, lambda i: (0, 0))

    names = ["mu", "w0", "w_up", "a0", "a_up", "g_up", "k_k", "k_a", "r_k"] + (["v0", "v_up"] if has_vres else [])
    for nm in names:
        in_specs.append(full(lp[nm]))
        args.append(lp[nm])
    if has_vres:
        in_specs.append(pl.BlockSpec((tp, width), lambda i: (i, 0)))
        args.append(vfirst)
    tok = pl.BlockSpec((tp, width), lambda i: (i, 0))
    if seq_mode:
        assert tp % RWKV_CHUNK == 0
        per_chunk = pl.BlockSpec((tp // RWKV_CHUNK * SUBLANES, width), lambda i: (i, 0))
        out_specs = [tok] * 9 + [per_chunk] + ([] if has_vres else [tok])
        out_shape = ([jax.ShapeDtypeStruct((rows, width), BF16)] * 9
                     + [jax.ShapeDtypeStruct((rows // RWKV_CHUNK * SUBLANES, width), F32)]
                     + ([] if has_vres else [jax.ShapeDtypeStruct((rows, width), F32)]))
    else:
        out_specs = [tok] * 7
        out_shape = [jax.ShapeDtypeStruct((rows, width), F32)] * 7
    return pl.pallas_call(
        functools.partial(_rwkv_prep_kernel, seq_mode=seq_mode, has_vres=has_vres,
                          tiles_per_seq=max(seq // tp, 1), width=width),
        grid=(rows // tp,),
        in_specs=in_specs,
        out_specs=out_specs,
        out_shape=out_shape,
        compiler_params=_params("parallel"),
        name="rwkv_prep",
    )(*args)


def _tri_inverse(a, eye):
    pw = -a
    inv = eye + pw
    for _ in range(a.shape[0].bit_length() - 2):
        pwb = _bf(pw)
        pw = _dot(pwb, pwb)
        inv = inv + _dot(_bf(inv), _bf(pw))
    return inv


def _rwkv_scan_kernel(khat_ref, rhat_ref, btil_ref, ktil_ref, bbar_ref, kbar_ref, v_ref, bonus_ref, g_ref,
                      dtot_ref, lnw_ref, lnb_ref, y_ref, s_ref,
                      wt_scr, u0_scr, arkv_scr, arb_scr, gt_scr, o_scr, *, n_chunks, chunk_group, post_rows):
    c, n = RWKV_CHUNK, RWKV_HEAD_DIM
    pairs = khat_ref.shape[1] // LANES
    row = lax.broadcasted_iota(jnp.int32, (c, c), 0)
    col = lax.broadcasted_iota(jnp.int32, (c, c), 1)
    strict, incl = row > col, row >= col
    eye = jnp.where(row == col, 1.0, 0.0)
    first = lax.broadcasted_iota(jnp.int32, (c, LANES), 1) < n
    first_b = jnp.where(first, 1.0, 0.0).astype(BF16)
    second_b = jnp.where(first, 0.0, 1.0).astype(BF16)
    vrow = lax.broadcasted_iota(jnp.int32, (LANES, LANES), 0) < n
    kcol = lax.broadcasted_iota(jnp.int32, (LANES, LANES), 1) < n
    same_head = vrow == kcol

    def pass1(gi, carry):
        for cj in range(chunk_group):
            ci = gi * chunk_group + cj
            rows = pl.ds(pl.multiple_of(ci * c, c), c)
            for p in range(pairs):
                ln = slice(p * LANES, (p + 1) * LANES)
                khat, rhat, v2 = khat_ref[rows, ln], rhat_ref[rows, ln], v_ref[rows, ln]
                x = jnp.concatenate([khat * first_b, khat * second_b, rhat * first_b, rhat * second_b], axis=0)
                ab = _dot_nt(x, btil_ref[rows, ln])
                ak = _dot_nt(x, ktil_ref[rows, ln])
                tinv, wt, akv, arkv = [], [], [], []
                for hh in range(2):
                    a_b = jnp.where(strict, ab[hh * c:(hh + 1) * c], 0.0)
                    a_k = jnp.where(strict, ak[hh * c:(hh + 1) * c], 0.0)
                    a_rb = jnp.where(incl, ab[(2 + hh) * c:(3 + hh) * c], 0.0)
                    a_rk = jnp.where(incl, ak[(2 + hh) * c:(3 + hh) * c], 0.0)
                    tinv.append(_bf(_tri_inverse(a_b, eye)))
                    wt.append(_dot(tinv[hh], khat))
                    akv.append(_dot(_bf(a_k), v2))
                    arkv.append(_dot(_bf(a_rk), v2))
                    arb_scr[ci, 2 * p + hh] = _bf(a_rb)
                akv_b = _bf(jnp.where(first, akv[0], akv[1]))
                wt_scr[rows, ln] = _bf(jnp.where(first, wt[0], wt[1]))
                u0_scr[rows, ln] = jnp.where(first, _dot(tinv[0], akv_b), _dot(tinv[1], akv_b))
                arkv_scr[rows, ln] = jnp.where(first, arkv[0], arkv[1])
                gt_scr[ci, p] = jnp.where(same_head, _dot_tn(v2, kbar_ref[rows, ln]), 0.0)
        return carry

    lax.fori_loop(0, n_chunks // chunk_group, pass1, 0)

    def pass2(ci, states):
        rows = pl.ds(pl.multiple_of(ci * c, c), c)
        drows = pl.ds(pl.multiple_of(ci * SUBLANES, SUBLANES), SUBLANES)
        new_states = []
        for p in range(pairs):
            ln = slice(p * LANES, (p + 1) * LANES)
            s = states[p]
            sb = _bf(s)
            u = _bf(-(_dot_nt(wt_scr[rows, ln], sb) + u0_scr[rows, ln]))
            o_scr[rows, ln] = (_dot_nt(rhat_ref[rows, ln], sb) + arkv_scr[rows, ln]
                               + jnp.where(first, _dot(arb_scr[ci, 2 * p], u), _dot(arb_scr[ci, 2 * p + 1], u)))
            decay = dtot_ref[drows, ln][0:1]
            new_states.append(s * decay + gt_scr[ci, p]
                              + jnp.where(same_head, _dot_tn(u, bbar_ref[rows, ln]), 0.0))
        return tuple(new_states)

    final = lax.fori_loop(0, n_chunks, pass2, tuple(jnp.zeros((LANES, LANES), F32) for _ in range(pairs)))
    for p in range(pairs):
        s_ref[0, 2 * p] = final[p][:n, :n]
        s_ref[0, 2 * p + 1] = final[p][n:, n:]

    gmat = _group_matrix()

    def post(ti, carry):
        rows = pl.ds(pl.multiple_of(ti * post_rows, post_rows), post_rows)
        o = o_scr[rows, :]
        oc = o - _group_sum(o, gmat) * (1.0 / n)
        var = _group_sum(oc * oc, gmat) * (1.0 / n)
        y = oc * lax.rsqrt(var + RWKV_LN_EPS) * lnw_ref[...] + lnb_ref[...]
        y_ref[rows, :] = ((y + bonus_ref[rows, :].astype(F32)) * g_ref[rows, :].astype(F32)).astype(y_ref.dtype)
        return carry

    lax.fori_loop(0, (n_chunks * c) // post_rows, post, 0)


def _rwkv_scan(feats, lnw, lnb, *, batch, seq, width):
    heads = width // RWKV_HEAD_DIM
    blk = _pick(width, (2 * LANES, LANES))
    hb = blk // RWKV_HEAD_DIM
    c = RWKV_CHUNK
    assert seq % c == 0
    n_chunks = seq // c
    tok = pl.BlockSpec((seq, blk), lambda b, h: (b, h))
    par = pl.BlockSpec((1, blk), lambda b, h: (0, h))
    return pl.pallas_call(
        functools.partial(_rwkv_scan_kernel, n_chunks=n_chunks, chunk_group=_pick(n_chunks, (2, 1)),
                          post_rows=_pick(seq, (256, 128, 64))),
        grid=(batch, width // blk),
        in_specs=[tok] * 9 + [pl.BlockSpec((n_chunks * SUBLANES, blk), lambda b, h: (b, h)), par, par],
        out_specs=[tok, pl.BlockSpec((1, hb, RWKV_HEAD_DIM, RWKV_HEAD_DIM), lambda b, h: (b, h, 0, 0))],
        out_shape=[jax.ShapeDtypeStruct((batch * seq, width), BF16),
                   jax.ShapeDtypeStruct((batch, heads, RWKV_HEAD_DIM, RWKV_HEAD_DIM), F32)],
        scratch_shapes=[pltpu.VMEM((seq, blk), BF16), pltpu.VMEM((seq, blk), F32), pltpu.VMEM((seq, blk), F32),
                        pltpu.VMEM((n_chunks, hb, c, c), BF16),
                        pltpu.VMEM((n_chunks, blk // LANES, LANES, LANES), F32),
                        pltpu.VMEM((seq, blk), F32)],
        compiler_params=_params("parallel", "parallel"),
        name="rwkv_scan",
    )(*feats[:10], lnw, lnb)


_W, _B, _K2, _V, _BR, _KR, _KK, _WR, _VP = range(9)


def _rwkv_sample_kernel(r_ref, lw_ref, k_ref, v_ref, kk_ref, b_ref, g_ref, s_ref, rk_ref, lnw_ref, lnb_ref,
                        y_ref, o_ref, stash, out_scr):
    n = RWKV_HEAD_DIM
    tt, width = r_ref.shape
    heads = width // n
    gmat = _group_matrix()
    r, k2, v, kk, b = r_ref[...], k_ref[...], v_ref[...], kk_ref[...], b_ref[...]
    w = jnp.exp(lw_ref[...])
    v_hi = _bf(v).astype(F32)
    v_lo = v - v_hi
    low = (lax.broadcasted_iota(jnp.int32, (tt, width), 1) % LANES) < n

    def up(x):
        return pltpu.roll(x, n, 1)

    def down(x):
        return pltpu.roll(x, width - n, 1)

    vecs = {_W: w, _B: b, _K2: k2, _V: v, _BR: _group_sum(b * r, gmat), _KR: _group_sum(k2 * r, gmat),
            _KK: kk, _WR: w * r}
    for idx, x in vecs.items():
        stash[2 * idx, :, 0, :] = x
        stash[2 * idx + 1, :, 0, :] = down(x)
    stash[2 * _VP, :, 0, :] = jnp.where(low, v_hi, up(v_lo))
    stash[2 * _VP + 1, :, 0, :] = jnp.where(low, down(v_hi), v_lo)
    out_scr[...] = jnp.zeros(out_scr.shape, F32)

    row = lax.broadcasted_iota(jnp.int32, (n, LANES), 0)
    col = lax.broadcasted_iota(jnp.int32, (n, LANES), 1)
    eye2 = jnp.where((col == row) | (col == row + n), 1.0, 0.0).astype(BF16)
    sub = lax.broadcasted_iota(jnp.int32, (SUBLANES, n), 0)

    def body(t, carry):
        for h in range(heads):
            par = h % 2
            ln = slice((h // 2) * LANES, (h // 2) * LANES + n)

            def vec(idx):
                return stash[2 * idx + par, t, :, ln]

            s = s_ref[t, h]
            sb = _bf(s)
            sa = -_dot_nt(sb, _bf(jnp.broadcast_to(vec(_KK), (n, n))))
            vp = stash[2 * _VP + par, t, :, (h // 2) * LANES:(h // 2 + 1) * LANES]
            v_col = _dot_nt(eye2, _bf(jnp.broadcast_to(vp, (n, LANES))))
            o_ref[t, h] = s * vec(_W) + sa * vec(_B) + v_col * vec(_K2)
            x = jnp.where(sub == 0, vec(_WR), jnp.where(sub == 1, vec(_KK), 0.0))
            sx = _dot_nt(_bf(x), sb)
            out_scr[par, t, :, ln] = sx[0:1] - sx[1:2] * vec(_BR) + vec(_V) * vec(_KR)
        return carry

    lax.fori_loop(0, tt, body, 0)
    o = jnp.where(low, out_scr[0, :, 0, :], up(out_scr[1, :, 0, :]))
    oc = o - _group_sum(o, gmat) * (1.0 / n)
    var = _group_sum(oc * oc, gmat) * (1.0 / n)
    y = oc * lax.rsqrt(var + RWKV_LN_EPS) * lnw_ref[...] + lnb_ref[...]
    bonus = _group_sum(r * k2 * rk_ref[...], gmat) * v
    y_ref[...] = ((y + bonus) * g_ref[...]).astype(y_ref.dtype)


def _rwkv_sample(feats, state, rk, lnw, lnb, *, width):
    bs, heads = state.shape[0], state.shape[1]
    tt = STEP_TOKENS
    assert bs % tt == 0
    tok = pl.BlockSpec((tt, width), lambda i: (i, 0))
    par = pl.BlockSpec((1, width), lambda i: (0, 0))
    st = pl.BlockSpec((tt, heads, RWKV_HEAD_DIM, RWKV_HEAD_DIM), lambda i: (i, 0, 0, 0))
    return pl.pallas_call(
        _rwkv_sample_kernel,
        grid=(bs // tt,),
        in_specs=[tok] * 7 + [st] + [par] * 3,
        out_specs=[tok, st],
        out_shape=[jax.ShapeDtypeStruct((bs, width), BF16), jax.ShapeDtypeStruct(state.shape, state.dtype)],
        scratch_shapes=[pltpu.VMEM((18, tt, 1, width), F32), pltpu.VMEM((2, tt, 1, width), F32)],
        compiler_params=_params("parallel"),
        name="rwkv_sample",
    )(*feats, state, rk, lnw, lnb)


def _rope_tables(pos):
    half = RET_HEAD_DIM // 2
    inv_freq = ROPE_BASE ** (-jnp.arange(half, dtype=F32) / half)
    ang = pos[:, None] * inv_freq[None, :]
    cos, sin = jnp.cos(ang), jnp.sin(ang)
    return jnp.concatenate([cos, cos], axis=-1), jnp.concatenate([-sin, sin], axis=-1)


def _pad_rows(w, start, total):
    return jnp.zeros((total, w.shape[1]), BF16).at[start:start + w.shape[0]].set(_bf(w))


def kernel(x_prompt, x_sample, state_ret, state_rwkv, state_shift, w_in, w_in_vres, mu_shift, mu_shift_vres, ret_ln_w, rwkv_w0, rwkv_w_up, rwkv_a0, rwkv_a_up, rwkv_g_up, rwkv_v0, rwkv_v_up, rwkv_k_k, rwkv_k_a, rwkv_r_k, rwkv_ln_w, rwkv_ln_b, w_out, g_attn, g_ffn, w_gate, w_up, w_down, g_final):
    batch, seq, d = x_prompt.shape
    bs = x_sample.shape[0]
    assert x_sample.shape[1] == 1
    depth = w_in.shape[0]
    mp = batch * seq
    ret_w = d // 2
    rw_w = d - ret_w
    ret_cols = 4 * ret_w
    main_cols = ret_cols + 3 * rw_w
    r_decay, r_aaa, r_gate = rwkv_w_up.shape[1], rwkv_a_up.shape[1], rwkv_g_up.shape[1]
    r_mv = rwkv_v_up.shape[1]
    o_a, o_g = r_decay, r_decay + r_aaa
    o_v = o_g + r_gate
    assert o_v + r_mv <= LORA_PAD and w_in.shape[2] == main_cols + o_v
    ret_heads = ret_w // RET_HEAD_DIM

    log_gamma = jnp.log1p(-jnp.exp2(-5.0 - jnp.arange(ret_heads, dtype=F32)))
    lg = jnp.broadcast_to(log_gamma[:, None, None], (ret_heads, 1, LANES))
    cos_p, sin_p = _rope_tables(jnp.arange(seq, dtype=F32))
    cos_s, sin_s = _rope_tables(PAST_LEN + jnp.arange(1, dtype=F32))

    h = jnp.concatenate([x_prompt.reshape(mp, d), x_sample.reshape(bs, d)], axis=0)
    vfirst_p = vfirst_s = None
    ret_p, rwkv_p, shift_p, ret_s, rwkv_s, shift_s = [], [], [], [], [], []
    for l in range(depth):
        lora_cols = [w_in[l][:, main_cols:]] + ([w_in_vres[l - 1]] if l > 0 else [])
        lora_mu = [mu_shift[l][3 * rw_w:]] + ([mu_shift_vres[l - 1]] if l > 0 else [])
        n_lora = sum(x.shape[1] for x in lora_cols)
        w_mix = _bf(jnp.concatenate([w_in[l][:, :main_cols]] + lora_cols
                                    + [jnp.zeros((d, LORA_PAD - n_lora), F32)], axis=1))
        mu = jnp.concatenate([mu_shift[l][:3 * rw_w]] + lora_mu + [jnp.zeros((LORA_PAD - n_lora,), F32)])
        lp = {
            "mu": mu.reshape(1, -1),
            "w0": rwkv_w0[l].reshape(1, -1), "w_up": _pad_rows(rwkv_w_up[l], 0, LORA_PAD),
            "a0": rwkv_a0[l].reshape(1, -1), "a_up": _pad_rows(rwkv_a_up[l], o_a, LORA_PAD),
            "g_up": _pad_rows(rwkv_g_up[l], o_g, LORA_PAD),
            "k_k": rwkv_k_k[l].reshape(1, -1), "k_a": rwkv_k_a[l].reshape(1, -1),
            "r_k": rwkv_r_k[l].reshape(1, -1),
        }
        if l > 0:
            lp["v0"] = rwkv_v0[l - 1].reshape(1, -1)
            lp["v_up"] = _pad_rows(rwkv_v_up[l - 1], o_v, LORA_PAD)
        lnw, lnb = rwkv_ln_w[l].reshape(1, -1), rwkv_ln_b[l].reshape(1, -1)
        ret_lnw = ret_ln_w[l].reshape(1, -1)

        xn = _rmsnorm(h, g_attn[l], BF16)
        proj = _matmul([(xn, w_mix, 0)], main_cols + LORA_PAD, F32, name="in_proj")
        h_last = jnp.concatenate([h[:mp].reshape(batch, seq, d)[:, -1], h[mp:]], axis=0)
        xn_last = _rmsnorm(h_last, g_attn[l], F32)
        shift_p.append(xn_last[:batch])
        shift_s.append(xn_last[batch:])
        prev_s = _matmul([(_bf(state_shift[l]), w_mix, 0)], 3 * rw_w + LORA_PAD, F32,
                         col_start=ret_cols, name="prev_proj")

        feats_p = _rwkv_prep(proj, None, lp, vfirst_p, row_start=0, rows=mp, seq=seq,
                             ret_cols=ret_cols, width=rw_w)
        feats_s = _rwkv_prep(proj, prev_s, lp, vfirst_s, row_start=mp, rows=bs, seq=1,
                             ret_cols=ret_cols, width=rw_w)
        if l == 0:
            vfirst_p, vfirst_s = feats_p[10], feats_s[3]

        y_ret_p, s_ret_p = _ret_prompt(proj, cos_p, sin_p, lg, ret_lnw, batch=batch, seq=seq, width=ret_w)
        y_ret_s, s_ret_s = _ret_sample(proj, state_ret[l], cos_s, sin_s, lg, ret_lnw, row_start=mp, width=ret_w)
        y_rw_p, s_rw_p = _rwkv_scan(feats_p, lnw, lnb, batch=batch, seq=seq, width=rw_w)
        y_rw_s, s_rw_s = _rwkv_sample(feats_s, state_rwkv[l], lp["r_k"], lnw, lnb, width=rw_w)
        ret_p.append(s_ret_p)
        ret_s.append(s_ret_s)
        rwkv_p.append(s_rw_p)
        rwkv_s.append(s_rw_s)

        y_ret = jnp.concatenate([y_ret_p, y_ret_s], axis=0)
        y_rw = jnp.concatenate([y_rw_p, y_rw_s], axis=0)
        w_o = _bf(w_out[l])
        h = _matmul([(y_ret, w_o, 0), (y_rw, w_o, ret_w)], d, F32, res=h, name="out_proj")
        hn = _rmsnorm(h, g_ffn[l], BF16)
        act = _swiglu(hn, _bf(w_gate[l]), _bf(w_up[l]))
        h = _matmul([(act, _bf(w_down[l]), 0)], d, F32, res=h, name="ffn_down")

    y_p = _rmsnorm(h, g_final, F32, row_start=0, rows=mp).reshape(batch, seq, d)
    y_s = _rmsnorm(h, g_final, F32, row_start=mp, rows=bs, tm=_pick(bs, (128, 64, 32, 16, 8))).reshape(bs, 1, d)
    return (y_p, y_s, jnp.stack(ret_p), jnp.stack(rwkv_p), jnp.stack(shift_p),
            jnp.stack(ret_s), jnp.stack(rwkv_s), jnp.stack(shift_s))
```

```python
import functools

import jax
import jax.numpy as jnp
from jax import lax
from jax.experimental import pallas as pl
from jax.experimental.pallas import tpu as pltpu

F32 = jnp.float32
BF16 = jnp.bfloat16

PAST_LEN = 16384
ROPE_BASE = 10000.0
RET_HEAD_DIM = 128
RET_CHUNK = 128
RWKV_HEAD_DIM = 64
RWKV_CHUNK = 64
NORM_EPS = 1e-6
RET_LN_EPS = 1e-5
RWKV_LN_EPS = 64e-5

LANES = 128
SUBLANES = 8
LORA_PAD = 512
VMEM_LIMIT_BYTES = 56 * 1024 * 1024
ROW_TILES = (832, 640, 512, 256, 128, 64, 32, 16)
STEP_TOKENS = 8


def _params(*sem):
    return pltpu.CompilerParams(dimension_semantics=sem, vmem_limit_bytes=VMEM_LIMIT_BYTES)


def _pick(n, candidates):
    for c in candidates:
        if n % c == 0:
            return c
    raise ValueError(f"no tile in {candidates} divides {n}")


def _dot(a, b):
    return jnp.dot(a, b, preferred_element_type=F32)


def _dot_nt(a, b):
    return lax.dot_general(a, b, (((1,), (1,)), ((), ())), preferred_element_type=F32)


def _dot_tn(a, b):
    return lax.dot_general(a, b, (((0,), (0,)), ((), ())), preferred_element_type=F32)


def _bf(x):
    return x.astype(BF16)


def _split2(x):
    hi = _bf(x)
    return hi, _bf(x - hi.astype(F32))


def _rmsnorm_kernel(x_ref, g_ref, o_ref):
    x = x_ref[...]
    y = x * lax.rsqrt(jnp.mean(x * x, axis=-1, keepdims=True) + NORM_EPS)
    o_ref[...] = (y * g_ref[...]).astype(o_ref.dtype)


def _rmsnorm(x, g, out_dtype, *, row_start=0, rows=None, tm=None):
    m, d = x.shape
    rows = m if rows is None else rows
    if tm is None:
        tm = rows if rows <= 640 else _pick(rows, (640, 512, 256, 128, 64, 32, 16))
    assert row_start % tm == 0 and rows % tm == 0
    off = row_start // tm
    return pl.pallas_call(
        _rmsnorm_kernel,
        grid=(rows // tm,),
        in_specs=[pl.BlockSpec((tm, d), lambda i: (i + off, 0)), pl.BlockSpec((1, d), lambda i: (0, 0))],
        out_specs=pl.BlockSpec((tm, d), lambda i: (i, 0)),
        out_shape=jax.ShapeDtypeStruct((rows, d), out_dtype),
        compiler_params=_params("parallel"),
        name="rmsnorm",
    )(x, g.reshape(1, d))


def _mm_kernel(*refs, n_pairs, has_res):
    o_ref = refs[-1]
    acc = None
    for a_ref, w_ref in zip(refs[:n_pairs], refs[n_pairs:2 * n_pairs]):
        d = _dot(a_ref[...], w_ref[...])
        acc = d if acc is None else acc + d
    if has_res:
        acc = acc + refs[2 * n_pairs][...]
    o_ref[...] = acc.astype(o_ref.dtype)


def _matmul(pairs, n, out_dtype, *, res=None, col_start=0, tn=None, name="matmul"):
    m = pairs[0][0].shape[0]
    tm = _pick(m, ROW_TILES)
    if tn is None:
        tn = _pick(n, (512, 256, 128))
    assert col_start % tn == 0
    coff = col_start // tn
    in_specs, args = [], []
    for a, _, _ in pairs:
        in_specs.append(pl.BlockSpec((tm, a.shape[1]), lambda i, j: (i, 0)))
        args.append(a)
    for a, w, k_start in pairs:
        k = a.shape[1]
        assert k_start % k == 0
        in_specs.append(pl.BlockSpec((k, tn), lambda i, j, kb=k_start // k: (kb, j + coff)))
        args.append(w)
    if res is not None:
        in_specs.append(pl.BlockSpec((tm, tn), lambda i, j: (i, j)))
        args.append(res)
    return pl.pallas_call(
        functools.partial(_mm_kernel, n_pairs=len(pairs), has_res=res is not None),
        grid=(m // tm, n // tn),
        in_specs=in_specs,
        out_specs=pl.BlockSpec((tm, tn), lambda i, j: (i, j)),
        out_shape=jax.ShapeDtypeStruct((m, n), out_dtype),
        compiler_params=_params("parallel", "arbitrary"),
        name=name,
    )(*args)


def _swiglu_kernel(a_ref, wg_ref, wu_ref, o_ref):
    a = a_ref[...]
    g = _dot(a, wg_ref[...])
    u = _dot(a, wu_ref[...])
    o_ref[...] = (g * jax.nn.sigmoid(g) * u).astype(o_ref.dtype)


def _swiglu(a, wg, wu):
    m, k = a.shape
    n = wg.shape[1]
    tm = _pick(m, ROW_TILES)
    tn = _pick(n, (512, 256, 128))
    return pl.pallas_call(
        _swiglu_kernel,
        grid=(m // tm, n // tn),
        in_specs=[pl.BlockSpec((tm, k), lambda i, j: (i, 0)),
                  pl.BlockSpec((k, tn), lambda i, j: (0, j)),
                  pl.BlockSpec((k, tn), lambda i, j: (0, j))],
        out_specs=pl.BlockSpec((tm, tn), lambda i, j: (i, j)),
        out_shape=jax.ShapeDtypeStruct((m, n), BF16),
        compiler_params=_params("parallel", "arbitrary"),
        name="swiglu",
    )(a, wg, wu)


def _rope(x, cos, sin_signed):
    return x * cos + pltpu.roll(x, RET_HEAD_DIM // 2, 1) * sin_signed


def _head_norm(y, eps):
    mu = jnp.mean(y, axis=-1, keepdims=True)
    yc = y - mu
    var = jnp.mean(yc * yc, axis=-1, keepdims=True)
    return yc * lax.rsqrt(var + eps)


def _ret_prompt_kernel(q_ref, k_ref, v_ref, g_ref, cos_ref, sin_ref, lg_ref, lnw_ref, y_ref, s_ref, *,
                       n_chunks, group):
    c, d = RET_CHUNK, RET_HEAD_DIM
    lg = lg_ref[0]
    row = lax.broadcasted_iota(jnp.int32, (c, c), 0)
    col = lax.broadcasted_iota(jnp.int32, (c, c), 1)
    diff = (row - col).astype(F32)
    decay_in = jnp.where(diff >= 0, jnp.exp(lg * jnp.maximum(diff, 0.0)), 0.0)
    ridx = lax.broadcasted_iota(jnp.int32, (c, d), 0).astype(F32)
    tail = jnp.exp((c - 1 - ridx) * lg)
    head = jnp.exp((ridx + 1) * lg)
    chunk_decay = jnp.exp(c * lg)
    lnw = lnw_ref[...]

    def body(gi, s):
        rows = [pl.ds(pl.multiple_of((gi * group + j) * c, c), c) for j in range(group)]
        cs = [(cos_ref[r, :], sin_ref[r, :]) for r in rows]
        q = [_rope(q_ref[r, :], co, si) for r, (co, si) in zip(rows, cs)]
        k = [_rope(k_ref[r, :], co, si) * (d ** -0.5) for r, (co, si) in zip(rows, cs)]
        vb = [_bf(v_ref[r, :]) for r in rows]
        scores = [_dot_nt(_bf(qj), _bf(kj)) * decay_in for qj, kj in zip(q, k)]
        kv = [_dot_tn(_bf(kj * tail), vj) for kj, vj in zip(k, vb)]
        states = [s]
        for j in range(group):
            states.append(states[j] * chunk_decay + kv[j])
        y = [_dot(_bf(sc), vj) + _dot(_bf(qj * head), _bf(sj))
             for sc, vj, qj, sj in zip(scores, vb, q, states)]
        for r, yj in zip(rows, y):
            g = g_ref[r, :]
            y_ref[r, :] = (g * jax.nn.sigmoid(g) * (_head_norm(yj, RET_LN_EPS) * lnw)).astype(y_ref.dtype)
        return states[group]

    s_ref[0, 0] = lax.fori_loop(0, n_chunks // group, body, jnp.zeros((d, d), F32))


def _ret_prompt(proj, cos, sin, lg, ln_w, *, batch, seq, width, total_rows):
    heads = width // RET_HEAD_DIM
    assert seq % RET_CHUNK == 0 and RET_CHUNK == LANES
    n_chunks = seq // RET_CHUNK

    def col(which):
        return pl.BlockSpec((seq, RET_HEAD_DIM), lambda b, h: (b, which * heads + h))

    return pl.pallas_call(
        functools.partial(_ret_prompt_kernel, n_chunks=n_chunks, group=_pick(n_chunks, (4, 2, 1))),
        grid=(batch, heads),
        in_specs=[col(0), col(1), col(2), col(3),
                  pl.BlockSpec((seq, RET_HEAD_DIM), lambda b, h: (0, 0)),
                  pl.BlockSpec((seq, RET_HEAD_DIM), lambda b, h: (0, 0)),
                  pl.BlockSpec((1, 1, LANES), lambda b, h: (h, 0, 0)),
                  pl.BlockSpec((1, RET_HEAD_DIM), lambda b, h: (0, h))],
        out_specs=[pl.BlockSpec((seq, RET_HEAD_DIM), lambda b, h: (b, h)),
                   pl.BlockSpec((1, 1, RET_HEAD_DIM, RET_HEAD_DIM), lambda b, h: (b, h, 0, 0))],
        out_shape=[jax.ShapeDtypeStruct((total_rows, width), BF16),
                   jax.ShapeDtypeStruct((batch, heads, RET_HEAD_DIM, RET_HEAD_DIM), F32)],
        compiler_params=_params("parallel", "parallel"),
        name="ret_prompt",
    )(proj, proj, proj, proj, cos, sin, lg, ln_w)


def _ret_sample_kernel(q_ref, k_ref, v_ref, g_ref, s_ref, cos_ref, sin_ref, lg_ref, lnw_ref, *rest):
    y_ref, o_ref, q_scr, kp_scr, v_scr, y_scr = rest[-6:]
    d = RET_HEAD_DIM
    tt, width = q_ref.shape
    heads = width // d
    cos, sin = cos_ref[...], sin_ref[...]
    v_scr[:, 0, :] = v_ref[...]
    for h in range(heads):
        ln = slice(h * d, (h + 1) * d)
        q_scr[:, 0, ln] = _rope(q_ref[:, ln], cos, sin)
        k = _rope(k_ref[:, ln], cos, sin) * (d ** -0.5)
        k_hi = _bf(k).astype(F32)
        kp_scr[:, 0, 2 * h * d:(2 * h + 1) * d] = k_hi
        kp_scr[:, 0, (2 * h + 1) * d:(2 * h + 2) * d] = k - k_hi
    row = lax.broadcasted_iota(jnp.int32, (d, 2 * d), 0)
    col = lax.broadcasted_iota(jnp.int32, (d, 2 * d), 1)
    eye2 = jnp.where((col == row) | (col == row + d), 1.0, 0.0).astype(BF16)

    def body(t, carry):
        lns = [slice(h * d, (h + 1) * d) for h in range(heads)]
        kp = [_bf(jnp.broadcast_to(kp_scr[t, :, 2 * h * d:(2 * h + 2) * d], (d, 2 * d))) for h in range(heads)]
        k_col = [_dot_nt(eye2, x) for x in kp]
        s_new = [s_ref[t, h] * jnp.exp(lg_ref[h]) + k_col[h] * v_scr[t, :, lns[h]] for h in range(heads)]
        for h in range(heads):
            o_ref[t, h] = s_new[h]
        y = [_dot(_bf(jnp.broadcast_to(q_scr[t, :, lns[h]], (SUBLANES, d))), _bf(s_new[h])) for h in range(heads)]
        for h in range(heads):
            y_scr[t, :, lns[h]] = y[h][0:1]
        return carry

    lax.fori_loop(0, tt, body, 0)
    for h in range(heads):
        ln = slice(h * d, (h + 1) * d)
        g = g_ref[:, ln]
        yn = _head_norm(y_scr[:, 0, ln], RET_LN_EPS) * lnw_ref[:, ln]
        y_ref[:, ln] = (g * jax.nn.sigmoid(g) * yn).astype(y_ref.dtype)


def _in_place(bufs, n_inputs):
    specs, args, aliases = [], [], {}
    for out_idx, buf in enumerate(bufs):
        if buf is not None:
            aliases[n_inputs + len(args)] = out_idx
            specs.append(pl.BlockSpec(memory_space=pl.ANY))
            args.append(buf)
    return specs, args, aliases


def _ret_sample(proj, states, layer, y_buf, new_states, cos, sin, lg, ln_w, *, row_start, width):
    bs, heads = states.shape[1], states.shape[2]
    tt = STEP_TOKENS
    assert row_start % tt == 0 and bs % tt == 0
    roff = row_start // tt

    def col(which):
        return pl.BlockSpec((tt, width), lambda i: (i + roff, which))

    st = pl.BlockSpec((None, tt, heads, RET_HEAD_DIM, RET_HEAD_DIM), lambda i: (layer, i, 0, 0, 0))
    in_specs = [col(0), col(1), col(2), col(3), st,
                pl.BlockSpec((1, RET_HEAD_DIM), lambda i: (0, 0)),
                pl.BlockSpec((1, RET_HEAD_DIM), lambda i: (0, 0)),
                pl.BlockSpec((heads, 1, LANES), lambda i: (0, 0, 0)),
                pl.BlockSpec((1, width), lambda i: (0, 0))]
    alias_specs, alias_args, aliases = _in_place([y_buf, new_states], len(in_specs))
    return pl.pallas_call(
        _ret_sample_kernel,
        grid=(bs // tt,),
        in_specs=in_specs + alias_specs,
        out_specs=[pl.BlockSpec((tt, width), lambda i: (i + roff, 0)), st],
        out_shape=[jax.ShapeDtypeStruct(y_buf.shape, y_buf.dtype), jax.ShapeDtypeStruct(states.shape, states.dtype)],
        scratch_shapes=[pltpu.VMEM((tt, 1, width), F32), pltpu.VMEM((tt, 1, 2 * width), F32),
                        pltpu.VMEM((tt, 1, width), F32), pltpu.VMEM((tt, 1, width), F32)],
        input_output_aliases=aliases,
        compiler_params=_params("parallel"),
        name="ret_sample",
    )(proj, proj, proj, proj, states, cos, sin, lg, ln_w, *alias_args)


def _softplus(x):
    return jnp.maximum(x, 0.0) + jnp.log1p(jnp.exp(-jnp.abs(x)))


def _group_matrix():
    r = lax.broadcasted_iota(jnp.int32, (LANES, LANES), 0) // RWKV_HEAD_DIM
    c = lax.broadcasted_iota(jnp.int32, (LANES, LANES), 1) // RWKV_HEAD_DIM
    return jnp.where(r == c, 1.0, 0.0).astype(BF16)


def _group_sum(x, gmat):
    outs = []
    for j in range(x.shape[1] // LANES):
        hi, lo = _split2(x[:, j * LANES:(j + 1) * LANES])
        outs.append(_dot(hi, gmat) + _dot(lo, gmat))
    return outs[0] if len(outs) == 1 else jnp.concatenate(outs, axis=1)


def _cumsum_rows(x, tri):
    hi = _bf(x)
    r1 = x - hi.astype(F32)
    mid = _bf(r1)
    lo = _bf(r1 - mid.astype(F32))
    return _dot(tri, hi) + _dot(tri, mid) + _dot(tri, lo)


def _rwkv_prep_kernel(*refs, seq_mode, has_vres, tiles_per_seq, width):
    it = iter(refs)
    cur = [next(it) for _ in range(4)]
    prev = [next(it) for _ in range(4)]
    mu_ref, w0_ref, wup_ref, a0_ref, aup_ref, gup_ref, kk_ref, ka_ref, rk_ref = (next(it) for _ in range(9))
    if has_vres:
        v0_ref, vup_ref, vfirst_ref = (next(it) for _ in range(3))
    outs = list(it)
    i = pl.program_id(0)

    def shifted(idx, mu):
        p = cur[idx][...]
        if seq_mode:
            carry = jnp.where(i % tiles_per_seq == 0, 0.0, prev[idx][SUBLANES - 1:SUBLANES, :])
            rowi = lax.broadcasted_iota(jnp.int32, p.shape, 0)
            p_prev = jnp.where(rowi == 0, carry, pltpu.roll(p, 1, 0))
        else:
            p_prev = prev[idx][...]
        return p + (p_prev - p) * mu

    w = width
    r = shifted(0, mu_ref[:, 0:w])
    k = shifted(1, mu_ref[:, w:2 * w])
    v = shifted(2, mu_ref[:, 2 * w:3 * w])
    lo = shifted(3, mu_ref[:, 3 * w:])
    lob = _bf(lo)
    w_raw = -_softplus(-(w0_ref[...] + _dot(_bf(jnp.tanh(lo)), wup_ref[...]))) - 0.5
    lw = -jnp.exp(w_raw)
    a = jax.nn.sigmoid(a0_ref[...] + _dot(lob, aup_ref[...]))
    g = _dot(_bf(jax.nn.sigmoid(lo)), gup_ref[...])
    if has_vres:
        v = v + (vfirst_ref[...] - v) * jax.nn.sigmoid(v0_ref[...] + _dot(lob, vup_ref[...]))
    k2 = k * (1.0 + (a - 1.0) * ka_ref[...])
    kk = k * kk_ref[...]
    gmat = _group_matrix()
    kk = kk / jnp.maximum(jnp.sqrt(_group_sum(kk * kk, gmat)), 1e-12)
    b = kk * a
    if not seq_mode:
        for o_ref, val in zip(outs, (r, lw, k2, v, kk, b, g)):
            o_ref[...] = val
        return

    tp, c = r.shape[0], RWKV_CHUNK
    row = lax.broadcasted_iota(jnp.int32, (tp, tp), 0)
    col = lax.broadcasted_iota(jnp.int32, (tp, tp), 1)
    tri = jnp.where(row >= col, jnp.where(row // c == col // c, 1.0, 0.0), 0.0).astype(BF16)
    cum = _cumsum_rows(lw, tri)
    last = [cum[j * c + c - 1:j * c + c, :] for j in range(tp // c)]
    tot = jnp.concatenate([jnp.broadcast_to(x, (c, w)) for x in last], axis=0)
    e_in = jnp.exp(-cum)
    e_out = jnp.exp(tot - cum)
    bonus = _group_sum(r * k2 * rk_ref[...], gmat) * v
    vals = (kk * jnp.exp(cum - lw), r * jnp.exp(cum), b * e_in, k2 * e_in, b * e_out, k2 * e_out, v, bonus, g)
    for o_ref, val in zip(outs, vals):
        o_ref[...] = val.astype(o_ref.dtype)
    outs[9][...] = jnp.concatenate([jnp.broadcast_to(jnp.exp(x), (SUBLANES, w)) for x in last], axis=0)
    if not has_vres:
        outs[10][...] = v


def _rwkv_prep(proj, prev_proj, lp, vfirst, *, row_start, rows, seq, ret_cols, width):
    seq_mode = prev_proj is None
    has_vres = vfirst is not None
    tp = _pick(seq, (256, 128, 64)) if seq_mode else _pick(rows, (128, 64, 32, 16, 8))
    assert row_start % tp == 0 and ret_cols % width == 0 and (ret_cols + 3 * width) % LORA_PAD == 0
    roff = row_start // tp
    cb = ret_cols // width
    lb = (ret_cols + 3 * width) // LORA_PAD
    widths = (width, width, width, LORA_PAD)
    cur_blocks = (cb, cb + 1, cb + 2, lb)
    in_specs, args = [], []
    for wd, blk in zip(widths, cur_blocks):
        in_specs.append(pl.BlockSpec((tp, wd), lambda i, blk=blk: (i + roff, blk)))
        args.append(proj)
    if seq_mode:
        for wd, blk in zip(widths, cur_blocks):
            in_specs.append(pl.BlockSpec(
                (SUBLANES, wd), lambda i, blk=blk: (jnp.maximum((i + roff) * (tp // SUBLANES) - 1, 0), blk)))
            args.append(proj)
    else:
        for wd, blk in zip(widths, (0, 1, 2, 3 * width // LORA_PAD)):
            in_specs.append(pl.BlockSpec((tp, wd), lambda i, blk=blk: (i, blk)))
            args.append(prev_proj)

    def full(x):
        return pl.BlockSpec(x.shape, lambda i: (0, 0))

    names = ["mu", "w0", "w_up", "a0", "a_up", "g_up", "k_k", "k_a", "r_k"] + (["v0", "v_up"] if has_vres else [])
    for nm in names:
        in_specs.append(full(lp[nm]))
        args.append(lp[nm])
    if has_vres:
        in_specs.append(pl.BlockSpec((tp, width), lambda i: (i, 0)))
        args.append(vfirst)
    tok = pl.BlockSpec((tp, width), lambda i: (i, 0))
    if seq_mode:
        assert tp % RWKV_CHUNK == 0
        per_chunk = pl.BlockSpec((tp // RWKV_CHUNK * SUBLANES, width), lambda i: (i, 0))
        out_specs = [tok] * 9 + [per_chunk] + ([] if has_vres else [tok])
        out_shape = ([jax.ShapeDtypeStruct((rows, width), BF16)] * 9
                     + [jax.ShapeDtypeStruct((rows // RWKV_CHUNK * SUBLANES, width), F32)]
                     + ([] if has_vres else [jax.ShapeDtypeStruct((rows, width), F32)]))
    else:
        out_specs = [tok] * 7
        out_shape = [jax.ShapeDtypeStruct((rows, width), F32)] * 7
    return pl.pallas_call(
        functools.partial(_rwkv_prep_kernel, seq_mode=seq_mode, has_vres=has_vres,
                          tiles_per_seq=max(seq // tp, 1), width=width),
        grid=(rows // tp,),
        in_specs=in_specs,
        out_specs=out_specs,
        out_shape=out_shape,
        compiler_params=_params("parallel"),
        name="rwkv_prep",
    )(*args)


def _rwkv_scan_kernel(khat_ref, rhat_ref, btil_ref, ktil_ref, bbar_ref, kbar_ref, v_ref, bonus_ref, g_ref,
                      dtot_ref, lnw_ref, lnb_ref, y_ref, s_ref,
                      r_scr, o0_scr, m_scr, n_scr, o_scr, *, n_chunks, chunk_group, post_rows):
    c, n = RWKV_CHUNK, RWKV_HEAD_DIM
    pairs = khat_ref.shape[1] // LANES
    row = lax.broadcasted_iota(jnp.int32, (c, c), 0)
    col = lax.broadcasted_iota(jnp.int32, (c, c), 1)
    strict, incl = row > col, row >= col
    eye = jnp.where(row == col, 1.0, 0.0)
    first = lax.broadcasted_iota(jnp.int32, (c, LANES), 1) < n
    first_b = jnp.where(first, 1.0, 0.0).astype(BF16)
    second_b = jnp.where(first, 0.0, 1.0).astype(BF16)
    vrow = lax.broadcasted_iota(jnp.int32, (LANES, LANES), 0) < n
    kcol = lax.broadcasted_iota(jnp.int32, (LANES, LANES), 1) < n
    same_head = vrow == kcol

    def lanes(p):
        return slice(p * LANES, (p + 1) * LANES)

    def pass1(gi, carry):
        inst = [(gi * chunk_group + cj, p) for cj in range(chunk_group) for p in range(pairs)]
        rows = [pl.ds(pl.multiple_of(ci * c, c), c) for ci, _ in inst]
        at = [(rows[i], lanes(p)) for i, (_, p) in enumerate(inst)]
        khat = [khat_ref[a] for a in at]
        rhat = [rhat_ref[a] for a in at]
        v2 = [v_ref[a] for a in at]
        x = [jnp.concatenate([kh * first_b, kh * second_b, rh * first_b, rh * second_b], axis=0)
             for kh, rh in zip(khat, rhat)]
        ab = [_dot_nt(xi, btil_ref[a]) for xi, a in zip(x, at)]
        ak = [_dot_nt(xi, ktil_ref[a]) for xi, a in zip(x, at)]
        heads2 = [(i, hh) for i in range(len(inst)) for hh in range(2)]
        pw = [-jnp.where(strict, ab[i][hh * c:(hh + 1) * c], 0.0) for i, hh in heads2]
        a_k = [_bf(jnp.where(strict, ak[i][hh * c:(hh + 1) * c], 0.0)) for i, hh in heads2]
        a_rk = [_bf(jnp.where(incl, ak[i][(2 + hh) * c:(3 + hh) * c], 0.0)) for i, hh in heads2]
        a_rb = [_bf(jnp.where(incl, ab[i][(2 + hh) * c:(3 + hh) * c], 0.0)) for i, hh in heads2]
        akv = [_dot(a_k[j], v2[i]) for j, (i, _) in enumerate(heads2)]
        arkv = [_dot(a_rk[j], v2[i]) for j, (i, _) in enumerate(heads2)]
        akv_b = [_bf(jnp.where(first, akv[2 * i], akv[2 * i + 1])) for i in range(len(inst))]
        tinv = [eye + q for q in pw]
        for _ in range(c.bit_length() - 2):
            pwb = [_bf(q) for q in pw]
            pw = [_dot(q, q) for q in pwb]
            tinv = [t + _dot(_bf(t), _bf(q)) for t, q in zip(tinv, pw)]
        tinv = [_bf(t) for t in tinv]
        wt = [_dot(tinv[j], khat[i]) for j, (i, _) in enumerate(heads2)]
        u0 = [_dot(tinv[j], akv_b[i]) for j, (i, _) in enumerate(heads2)]
        wt_b = [_bf(jnp.where(first, wt[2 * i], wt[2 * i + 1])) for i in range(len(inst))]
        u0_b = [_bf(jnp.where(first, u0[2 * i], u0[2 * i + 1])) for i in range(len(inst))]
        bbar = [bbar_ref[a] for a in at]
        m = [_dot_tn(wt_b[i], bbar[i]) for i in range(len(inst))]
        nn = [_dot_tn(jnp.concatenate([v2[i], -u0_b[i]], axis=0),
                      jnp.concatenate([kbar_ref[at[i]], bbar[i]], axis=0)) for i in range(len(inst))]
        rw = [_dot(a_rb[j], wt_b[i]) for j, (i, _) in enumerate(heads2)]
        ru = [_dot(a_rb[j], u0_b[i]) for j, (i, _) in enumerate(heads2)]
        for i, (ci, p) in enumerate(inst):
            m_scr[ci, p] = _bf(jnp.where(same_head, -m[i], 0.0))
            n_scr[ci, p] = jnp.where(same_head, nn[i], 0.0)
            r_scr[at[i]] = _bf(rhat[i].astype(F32) - jnp.where(first, rw[2 * i], rw[2 * i + 1]))
            o0_scr[at[i]] = jnp.where(first, arkv[2 * i] - ru[2 * i], arkv[2 * i + 1] - ru[2 * i + 1])
        return carry

    lax.fori_loop(0, n_chunks // chunk_group, pass1, 0)

    def pass2(ci, states):
        rows = pl.ds(pl.multiple_of(ci * c, c), c)
        drows = pl.ds(pl.multiple_of(ci * SUBLANES, SUBLANES), SUBLANES)
        sb = [_bf(s) for s in states]
        ds = [_dot(sb[p], m_scr[ci, p]) for p in range(pairs)]
        for p in range(pairs):
            at = (rows, lanes(p))
            o_scr[at] = _dot_nt(r_scr[at], sb[p]) + o0_scr[at]
        return tuple(states[p] * dtot_ref[drows, lanes(p)][0:1] + ds[p] + n_scr[ci, p] for p in range(pairs))

    final = lax.fori_loop(0, n_chunks, pass2, tuple(jnp.zeros((LANES, LANES), F32) for _ in range(pairs)))
    for p in range(pairs):
        s_ref[0, 2 * p] = final[p][:n, :n]
        s_ref[0, 2 * p + 1] = final[p][n:, n:]

    gmat = _group_matrix()

    def post(ti, carry):
        rows = pl.ds(pl.multiple_of(ti * post_rows, post_rows), post_rows)
        o = o_scr[rows, :]
        oc = o - _group_sum(o, gmat) * (1.0 / n)
        var = _group_sum(oc * oc, gmat) * (1.0 / n)
        y = oc * lax.rsqrt(var + RWKV_LN_EPS) * lnw_ref[...] + lnb_ref[...]
        y_ref[rows, :] = ((y + bonus_ref[rows, :].astype(F32)) * g_ref[rows, :].astype(F32)).astype(y_ref.dtype)
        return carry

    lax.fori_loop(0, (n_chunks * c) // post_rows, post, 0)


def _rwkv_scan(feats, lnw, lnb, *, batch, seq, width, total_rows):
    heads = width // RWKV_HEAD_DIM
    blk = _pick(width, (2 * LANES, LANES))
    hb = blk // RWKV_HEAD_DIM
    c = RWKV_CHUNK
    assert seq % c == 0
    n_chunks = seq // c
    tok = pl.BlockSpec((seq, blk), lambda b, h: (b, h))
    par = pl.BlockSpec((1, blk), lambda b, h: (0, h))
    return pl.pallas_call(
        functools.partial(_rwkv_scan_kernel, n_chunks=n_chunks, chunk_group=_pick(n_chunks, (4, 2, 1)),
                          post_rows=_pick(seq, (256, 128, 64))),
        grid=(batch, width // blk),
        in_specs=[tok] * 9 + [pl.BlockSpec((n_chunks * SUBLANES, blk), lambda b, h: (b, h)), par, par],
        out_specs=[tok, pl.BlockSpec((1, hb, RWKV_HEAD_DIM, RWKV_HEAD_DIM), lambda b, h: (b, h, 0, 0))],
        out_shape=[jax.ShapeDtypeStruct((total_rows, width), BF16),
                   jax.ShapeDtypeStruct((batch, heads, RWKV_HEAD_DIM, RWKV_HEAD_DIM), F32)],
        scratch_shapes=[pltpu.VMEM((seq, blk), BF16), pltpu.VMEM((seq, blk), F32),
                        pltpu.VMEM((n_chunks, blk // LANES, LANES, LANES), BF16),
                        pltpu.VMEM((n_chunks, blk // LANES, LANES, LANES), F32),
                        pltpu.VMEM((seq, blk), F32)],
        compiler_params=_params("parallel", "parallel"),
        name="rwkv_scan",
    )(*feats[:10], lnw, lnb)


_W, _B, _K2, _V, _BR, _KR, _KK, _WR, _VP = range(9)


def _rwkv_sample_kernel(r_ref, lw_ref, k_ref, v_ref, kk_ref, b_ref, g_ref, s_ref, rk_ref, lnw_ref, lnb_ref, *rest):
    y_ref, o_ref, stash, out_scr = rest[-4:]
    n = RWKV_HEAD_DIM
    tt, width = r_ref.shape
    heads = width // n
    gmat = _group_matrix()
    r, k2, v, kk, b = r_ref[...], k_ref[...], v_ref[...], kk_ref[...], b_ref[...]
    w = jnp.exp(lw_ref[...])
    v_hi = _bf(v).astype(F32)
    v_lo = v - v_hi
    low = (lax.broadcasted_iota(jnp.int32, (tt, width), 1) % LANES) < n

    def up(x):
        return pltpu.roll(x, n, 1)

    def down(x):
        return pltpu.roll(x, width - n, 1)

    vecs = {_W: w, _B: b, _K2: k2, _V: v, _BR: _group_sum(b * r, gmat), _KR: _group_sum(k2 * r, gmat),
            _KK: kk, _WR: w * r}
    for idx, x in vecs.items():
        stash[2 * idx, :, 0, :] = x
        stash[2 * idx + 1, :, 0, :] = down(x)
    stash[2 * _VP, :, 0, :] = jnp.where(low, v_hi, up(v_lo))
    stash[2 * _VP + 1, :, 0, :] = jnp.where(low, down(v_hi), v_lo)
    out_scr[...] = jnp.zeros(out_scr.shape, F32)

    row = lax.broadcasted_iota(jnp.int32, (n, LANES), 0)
    col = lax.broadcasted_iota(jnp.int32, (n, LANES), 1)
    eye2 = jnp.where((col == row) | (col == row + n), 1.0, 0.0).astype(BF16)
    sub = lax.broadcasted_iota(jnp.int32, (SUBLANES, n), 0)

    def body(t, carry):
        def vec(idx, h):
            return stash[2 * idx + h % 2, t, :, (h // 2) * LANES:(h // 2) * LANES + n]

        hs = range(heads)
        s = [s_ref[t, h] for h in hs]
        sb = [_bf(x) for x in s]
        sa = [_dot_nt(sb[h], _bf(jnp.broadcast_to(vec(_KK, h), (n, n)))) for h in hs]
        vp = [stash[2 * _VP + h % 2, t, :, (h // 2) * LANES:(h // 2 + 1) * LANES] for h in hs]
        v_col = [_dot_nt(eye2, _bf(jnp.broadcast_to(x, (n, LANES)))) for x in vp]
        x = [_bf(jnp.where(sub == 0, vec(_WR, h), jnp.where(sub == 1, vec(_KK, h), 0.0))) for h in hs]
        sx = [_dot_nt(x[h], sb[h]) for h in hs]
        for h in hs:
            o_ref[t, h] = s[h] * vec(_W, h) - sa[h] * vec(_B, h) + v_col[h] * vec(_K2, h)
            out_scr[h % 2, t, :, (h // 2) * LANES:(h // 2) * LANES + n] = (
                sx[h][0:1] - sx[h][1:2] * vec(_BR, h) + vec(_V, h) * vec(_KR, h))
        return carry

    lax.fori_loop(0, tt, body, 0)
    o = jnp.where(low, out_scr[0, :, 0, :], up(out_scr[1, :, 0, :]))
    oc = o - _group_sum(o, gmat) * (1.0 / n)
    var = _group_sum(oc * oc, gmat) * (1.0 / n)
    y = oc * lax.rsqrt(var + RWKV_LN_EPS) * lnw_ref[...] + lnb_ref[...]
    bonus = _group_sum(r * k2 * rk_ref[...], gmat) * v
    y_ref[...] = ((y + bonus) * g_ref[...]).astype(y_ref.dtype)


def _rwkv_sample(feats, states, layer, y_buf, new_states, rk, lnw, lnb, *, row_start, width):
    bs, heads = states.shape[1], states.shape[2]
    tt = STEP_TOKENS
    assert bs % tt == 0 and row_start % tt == 0
    roff = row_start // tt
    tok = pl.BlockSpec((tt, width), lambda i: (i, 0))
    par = pl.BlockSpec((1, width), lambda i: (0, 0))
    st = pl.BlockSpec((None, tt, heads, RWKV_HEAD_DIM, RWKV_HEAD_DIM), lambda i: (layer, i, 0, 0, 0))
    in_specs = [tok] * 7 + [st] + [par] * 3
    alias_specs, alias_args, aliases = _in_place([y_buf, new_states], len(in_specs))
    return pl.pallas_call(
        _rwkv_sample_kernel,
        grid=(bs // tt,),
        in_specs=in_specs + alias_specs,
        out_specs=[pl.BlockSpec((tt, width), lambda i: (i + roff, 0)), st],
        out_shape=[jax.ShapeDtypeStruct(y_buf.shape, y_buf.dtype), jax.ShapeDtypeStruct(states.shape, states.dtype)],
        scratch_shapes=[pltpu.VMEM((18, tt, 1, width), F32), pltpu.VMEM((2, tt, 1, width), F32)],
        input_output_aliases=aliases,
        compiler_params=_params("parallel"),
        name="rwkv_sample",
    )(*feats, states, rk, lnw, lnb, *alias_args)


def _rope_tables(pos):
    half = RET_HEAD_DIM // 2
    inv_freq = ROPE_BASE ** (-jnp.arange(half, dtype=F32) / half)
    ang = pos[:, None] * inv_freq[None, :]
    cos, sin = jnp.cos(ang), jnp.sin(ang)
    return jnp.concatenate([cos, cos], axis=-1), jnp.concatenate([-sin, sin], axis=-1)


def _pad_rows(w, start, total):
    return jnp.zeros((total, w.shape[1]), BF16).at[start:start + w.shape[0]].set(_bf(w))


def kernel(x_prompt, x_sample, state_ret, state_rwkv, state_shift, w_in, w_in_vres, mu_shift, mu_shift_vres, ret_ln_w, rwkv_w0, rwkv_w_up, rwkv_a0, rwkv_a_up, rwkv_g_up, rwkv_v0, rwkv_v_up, rwkv_k_k, rwkv_k_a, rwkv_r_k, rwkv_ln_w, rwkv_ln_b, w_out, g_attn, g_ffn, w_gate, w_up, w_down, g_final):
    batch, seq, d = x_prompt.shape
    bs = x_sample.shape[0]
    assert x_sample.shape[1] == 1
    depth = w_in.shape[0]
    mp = batch * seq
    ret_w = d // 2
    rw_w = d - ret_w
    ret_cols = 4 * ret_w
    main_cols = ret_cols + 3 * rw_w
    r_decay, r_aaa, r_gate = rwkv_w_up.shape[1], rwkv_a_up.shape[1], rwkv_g_up.shape[1]
    r_mv = rwkv_v_up.shape[1]
    o_a, o_g = r_decay, r_decay + r_aaa
    o_v = o_g + r_gate
    assert o_v + r_mv <= LORA_PAD and w_in.shape[2] == main_cols + o_v
    ret_heads = ret_w // RET_HEAD_DIM

    log_gamma = jnp.log1p(-jnp.exp2(-5.0 - jnp.arange(ret_heads, dtype=F32)))
    lg = jnp.broadcast_to(log_gamma[:, None, None], (ret_heads, 1, LANES))
    cos_p, sin_p = _rope_tables(jnp.arange(seq, dtype=F32))
    cos_s, sin_s = _rope_tables(PAST_LEN + jnp.arange(1, dtype=F32))

    h = jnp.concatenate([x_prompt.reshape(mp, d), x_sample.reshape(bs, d)], axis=0)
    vfirst_p = vfirst_s = None
    ret_p, rwkv_p, shift_p, shift_s = [], [], [], []
    ret_s = rwkv_s = None
    for l in range(depth):
        lora_cols = [w_in[l][:, main_cols:]] + ([w_in_vres[l - 1]] if l > 0 else [])
        lora_mu = [mu_shift[l][3 * rw_w:]] + ([mu_shift_vres[l - 1]] if l > 0 else [])
        n_lora = sum(x.shape[1] for x in lora_cols)
        w_mix = _bf(jnp.concatenate([w_in[l][:, :main_cols]] + lora_cols
                                    + [jnp.zeros((d, LORA_PAD - n_lora), F32)], axis=1))
        mu = jnp.concatenate([mu_shift[l][:3 * rw_w]] + lora_mu + [jnp.zeros((LORA_PAD - n_lora,), F32)])
        lp = {
            "mu": mu.reshape(1, -1),
            "w0": rwkv_w0[l].reshape(1, -1), "w_up": _pad_rows(rwkv_w_up[l], 0, LORA_PAD),
            "a0": rwkv_a0[l].reshape(1, -1), "a_up": _pad_rows(rwkv_a_up[l], o_a, LORA_PAD),
            "g_up": _pad_rows(rwkv_g_up[l], o_g, LORA_PAD),
            "k_k": rwkv_k_k[l].reshape(1, -1), "k_a": rwkv_k_a[l].reshape(1, -1),
            "r_k": rwkv_r_k[l].reshape(1, -1),
        }
        if l > 0:
            lp["v0"] = rwkv_v0[l - 1].reshape(1, -1)
            lp["v_up"] = _pad_rows(rwkv_v_up[l - 1], o_v, LORA_PAD)
        lnw, lnb = rwkv_ln_w[l].reshape(1, -1), rwkv_ln_b[l].reshape(1, -1)
        ret_lnw = ret_ln_w[l].reshape(1, -1)

        xn = _rmsnorm(h, g_attn[l], BF16)
        proj = _matmul([(xn, w_mix, 0)], main_cols + LORA_PAD, F32, name="in_proj")
        h_last = jnp.concatenate([h[:mp].reshape(batch, seq, d)[:, -1], h[mp:]], axis=0)
        xn_last = _rmsnorm(h_last, g_attn[l], F32)
        shift_p.append(xn_last[:batch])
        shift_s.append(xn_last[batch:])
        prev_s = _matmul([(_bf(state_shift[l]), w_mix, 0)], 3 * rw_w + LORA_PAD, F32,
                         col_start=ret_cols, name="prev_proj")

        feats_p = _rwkv_prep(proj, None, lp, vfirst_p, row_start=0, rows=mp, seq=seq,
                             ret_cols=ret_cols, width=rw_w)
        feats_s = _rwkv_prep(proj, prev_s, lp, vfirst_s, row_start=mp, rows=bs, seq=1,
                             ret_cols=ret_cols, width=rw_w)
        if l == 0:
            vfirst_p, vfirst_s = feats_p[10], feats_s[3]

        y_ret, s_ret_p = _ret_prompt(proj, cos_p, sin_p, lg, ret_lnw, batch=batch, seq=seq, width=ret_w,
                                     total_rows=mp + bs)
        y_ret, ret_s = _ret_sample(proj, state_ret, l, y_ret, ret_s, cos_s, sin_s, lg, ret_lnw,
                                   row_start=mp, width=ret_w)
        y_rw, s_rw_p = _rwkv_scan(feats_p, lnw, lnb, batch=batch, seq=seq, width=rw_w, total_rows=mp + bs)
        y_rw, rwkv_s = _rwkv_sample(feats_s, state_rwkv, l, y_rw, rwkv_s, lp["r_k"], lnw, lnb,
                                    row_start=mp, width=rw_w)
        ret_p.append(s_ret_p)
        rwkv_p.append(s_rw_p)

        w_o = _bf(w_out[l])
        h = _matmul([(y_ret, w_o, 0), (y_rw, w_o, ret_w)], d, F32, res=h, name="out_proj")
        hn = _rmsnorm(h, g_ffn[l], BF16)
        act = _swiglu(hn, _bf(w_gate[l]), _bf(w_up[l]))
        h = _matmul([(act, _bf(w_down[l]), 0)], d, F32, res=h, name="ffn_down")

    y_p = _rmsnorm(h, g_final, F32, row_start=0, rows=mp).reshape(batch, seq, d)
    y_s = _rmsnorm(h, g_final, F32, row_start=mp, rows=bs, tm=_pick(bs, (128, 64, 32, 16, 8))).reshape(bs, 1, d)
    return (y_p, y_s, jnp.stack(ret_p), jnp.stack(rwkv_p), jnp.stack(shift_p),
            ret_s, rwkv_s, jnp.stack(shift_s))
```

```python
import functools

import jax
import jax.numpy as jnp
from jax import lax
from jax.experimental import pallas as pl
from jax.experimental.pallas import tpu as pltpu

F32 = jnp.float32
BF16 = jnp.bfloat16

PAST_LEN = 16384
ROPE_BASE = 10000.0
RET_HEAD_DIM = 128
RET_CHUNK = 128
RWKV_HEAD_DIM = 64
RWKV_CHUNK = 64
NORM_EPS = 1e-6
RET_LN_EPS = 1e-5
RWKV_LN_EPS = 64e-5

LANES = 128
SUBLANES = 8
LORA_PAD = 512
VMEM_LIMIT_BYTES = 56 * 1024 * 1024
ROW_TILES = (832, 640, 512, 256, 128, 64, 32, 16)
STEP_TOKENS = 8


def _params(*sem):
    return pltpu.CompilerParams(dimension_semantics=sem, vmem_limit_bytes=VMEM_LIMIT_BYTES)


def _pick(n, candidates):
    for c in candidates:
        if n % c == 0:
            return c
    raise ValueError(f"no tile in {candidates} divides {n}")


def _dot(a, b):
    return jnp.dot(a, b, preferred_element_type=F32)


def _dot_nt(a, b):
    return lax.dot_general(a, b, (((1,), (1,)), ((), ())), preferred_element_type=F32)


def _dot_tn(a, b):
    return lax.dot_general(a, b, (((0,), (0,)), ((), ())), preferred_element_type=F32)


def _bf(x):
    return x.astype(BF16)


def _split2(x):
    hi = _bf(x)
    return hi, _bf(x - hi.astype(F32))


def _rmsnorm_kernel(x_ref, g_ref, o_ref):
    x = x_ref[...]
    y = x * lax.rsqrt(jnp.mean(x * x, axis=-1, keepdims=True) + NORM_EPS)
    o_ref[...] = (y * g_ref[...]).astype(o_ref.dtype)


def _rmsnorm(x, g, out_dtype, *, row_start=0, rows=None, tm=None):
    m, d = x.shape
    rows = m if rows is None else rows
    if tm is None:
        tm = rows if rows <= 640 else _pick(rows, (640, 512, 256, 128, 64, 32, 16))
    assert row_start % tm == 0 and rows % tm == 0
    off = row_start // tm
    return pl.pallas_call(
        _rmsnorm_kernel,
        grid=(rows // tm,),
        in_specs=[pl.BlockSpec((tm, d), lambda i: (i + off, 0)), pl.BlockSpec((1, d), lambda i: (0, 0))],
        out_specs=pl.BlockSpec((tm, d), lambda i: (i, 0)),
        out_shape=jax.ShapeDtypeStruct((rows, d), out_dtype),
        compiler_params=_params("parallel"),
        name="rmsnorm",
    )(x, g.reshape(1, d))


def _mm_kernel(*refs, n_pairs, has_res):
    o_ref = refs[-1]
    acc = None
    for a_ref, w_ref in zip(refs[:n_pairs], refs[n_pairs:2 * n_pairs]):
        d = _dot(a_ref[...], w_ref[...])
        acc = d if acc is None else acc + d
    if has_res:
        acc = acc + refs[2 * n_pairs][...]
    o_ref[...] = acc.astype(o_ref.dtype)


def _matmul(pairs, layer, n, out_dtype, *, res=None, col_start=0, tn=None, name="matmul"):
    m = pairs[0][0].shape[0]
    tm = _pick(m, ROW_TILES)
    if tn is None:
        tn = _pick(n, (512, 256, 128))
    assert col_start % tn == 0
    coff = col_start // tn
    in_specs, args = [], []
    for a, _, _ in pairs:
        in_specs.append(pl.BlockSpec((tm, a.shape[1]), lambda i, j: (i, 0)))
        args.append(a)
    for a, w, k_start in pairs:
        k = a.shape[1]
        assert k_start % k == 0
        in_specs.append(pl.BlockSpec((None, k, tn), lambda i, j, kb=k_start // k: (layer, kb, j + coff)))
        args.append(w)
    if res is not None:
        in_specs.append(pl.BlockSpec((tm, tn), lambda i, j: (i, j)))
        args.append(res)
    return pl.pallas_call(
        functools.partial(_mm_kernel, n_pairs=len(pairs), has_res=res is not None),
        grid=(m // tm, n // tn),
        in_specs=in_specs,
        out_specs=pl.BlockSpec((tm, tn), lambda i, j: (i, j)),
        out_shape=jax.ShapeDtypeStruct((m, n), out_dtype),
        compiler_params=_params("parallel", "arbitrary"),
        name=name,
    )(*args)


def _swiglu_kernel(a_ref, wg_ref, wu_ref, o_ref):
    a = a_ref[...]
    g = _dot(a, wg_ref[...])
    u = _dot(a, wu_ref[...])
    o_ref[...] = (g * jax.nn.sigmoid(g) * u).astype(o_ref.dtype)


def _swiglu(a, wg, wu, layer):
    m, k = a.shape
    n = wg.shape[2]
    tm = _pick(m, ROW_TILES)
    tn = _pick(n, (512, 256, 128))
    return pl.pallas_call(
        _swiglu_kernel,
        grid=(m // tm, n // tn),
        in_specs=[pl.BlockSpec((tm, k), lambda i, j: (i, 0)),
                  pl.BlockSpec((None, k, tn), lambda i, j: (layer, 0, j)),
                  pl.BlockSpec((None, k, tn), lambda i, j: (layer, 0, j))],
        out_specs=pl.BlockSpec((tm, tn), lambda i, j: (i, j)),
        out_shape=jax.ShapeDtypeStruct((m, n), BF16),
        compiler_params=_params("parallel", "arbitrary"),
        name="swiglu",
    )(a, wg, wu)


def _rope(x, cos, sin_signed):
    return x * cos + pltpu.roll(x, RET_HEAD_DIM // 2, 1) * sin_signed


def _head_norm(y, eps):
    mu = jnp.mean(y, axis=-1, keepdims=True)
    yc = y - mu
    var = jnp.mean(yc * yc, axis=-1, keepdims=True)
    return yc * lax.rsqrt(var + eps)


def _ret_prompt_kernel(q_ref, k_ref, v_ref, g_ref, cos_ref, sin_ref, lg_ref, lnw_ref, y_ref, s_ref, *,
                       n_chunks, group):
    c, d = RET_CHUNK, RET_HEAD_DIM
    lg = lg_ref[0]
    row = lax.broadcasted_iota(jnp.int32, (c, c), 0)
    col = lax.broadcasted_iota(jnp.int32, (c, c), 1)
    diff = (row - col).astype(F32)
    decay_in = jnp.where(diff >= 0, jnp.exp(lg * jnp.maximum(diff, 0.0)), 0.0)
    ridx = lax.broadcasted_iota(jnp.int32, (c, d), 0).astype(F32)
    tail = jnp.exp((c - 1 - ridx) * lg)
    head = jnp.exp((ridx + 1) * lg)
    chunk_decay = jnp.exp(c * lg)
    lnw = lnw_ref[...]

    def body(gi, s):
        rows = [pl.ds(pl.multiple_of((gi * group + j) * c, c), c) for j in range(group)]
        cs = [(cos_ref[r, :], sin_ref[r, :]) for r in rows]
        q = [_rope(q_ref[r, :], co, si) for r, (co, si) in zip(rows, cs)]
        k = [_rope(k_ref[r, :], co, si) * (d ** -0.5) for r, (co, si) in zip(rows, cs)]
        vb = [_bf(v_ref[r, :]) for r in rows]
        scores = [_dot_nt(_bf(qj), _bf(kj)) * decay_in for qj, kj in zip(q, k)]
        kv = [_dot_tn(_bf(kj * tail), vj) for kj, vj in zip(k, vb)]
        states = [s]
        for j in range(group):
            states.append(states[j] * chunk_decay + kv[j])
        y = [_dot(_bf(sc), vj) + _dot(_bf(qj * head), _bf(sj))
             for sc, vj, qj, sj in zip(scores, vb, q, states)]
        for r, yj in zip(rows, y):
            g = g_ref[r, :]
            y_ref[r, :] = (g * jax.nn.sigmoid(g) * (_head_norm(yj, RET_LN_EPS) * lnw)).astype(y_ref.dtype)
        return states[group]

    s_ref[0, 0] = lax.fori_loop(0, n_chunks // group, body, jnp.zeros((d, d), F32))


def _ret_prompt(proj, cos, sin, lg, ln_w, *, batch, seq, width, total_rows):
    heads = width // RET_HEAD_DIM
    assert seq % RET_CHUNK == 0 and RET_CHUNK == LANES
    n_chunks = seq // RET_CHUNK

    def col(which):
        return pl.BlockSpec((seq, RET_HEAD_DIM), lambda b, h: (b, which * heads + h))

    return pl.pallas_call(
        functools.partial(_ret_prompt_kernel, n_chunks=n_chunks, group=_pick(n_chunks, (4, 2, 1))),
        grid=(batch, heads),
        in_specs=[col(0), col(1), col(2), col(3),
                  pl.BlockSpec((seq, RET_HEAD_DIM), lambda b, h: (0, 0)),
                  pl.BlockSpec((seq, RET_HEAD_DIM), lambda b, h: (0, 0)),
                  pl.BlockSpec((1, 1, LANES), lambda b, h: (h, 0, 0)),
                  pl.BlockSpec((1, RET_HEAD_DIM), lambda b, h: (0, h))],
        out_specs=[pl.BlockSpec((seq, RET_HEAD_DIM), lambda b, h: (b, h)),
                   pl.BlockSpec((1, 1, RET_HEAD_DIM, RET_HEAD_DIM), lambda b, h: (b, h, 0, 0))],
        out_shape=[jax.ShapeDtypeStruct((total_rows, width), BF16),
                   jax.ShapeDtypeStruct((batch, heads, RET_HEAD_DIM, RET_HEAD_DIM), F32)],
        compiler_params=_params("parallel", "parallel"),
        name="ret_prompt",
    )(proj, proj, proj, proj, cos, sin, lg, ln_w)


def _ret_sample_kernel(q_ref, k_ref, v_ref, g_ref, s_ref, cos_ref, sin_ref, lg_ref, lnw_ref, *rest):
    y_ref, o_ref, q_scr, kp_scr, v_scr, y_scr = rest[-6:]
    d = RET_HEAD_DIM
    tt, width = q_ref.shape
    heads = width // d
    cos, sin = cos_ref[...], sin_ref[...]
    v_scr[:, 0, :] = v_ref[...]
    for h in range(heads):
        ln = slice(h * d, (h + 1) * d)
        q_scr[:, 0, ln] = _rope(q_ref[:, ln], cos, sin)
        k = _rope(k_ref[:, ln], cos, sin) * (d ** -0.5)
        k_hi = _bf(k).astype(F32)
        kp_scr[:, 0, 2 * h * d:(2 * h + 1) * d] = k_hi
        kp_scr[:, 0, (2 * h + 1) * d:(2 * h + 2) * d] = k - k_hi
    row = lax.broadcasted_iota(jnp.int32, (d, 2 * d), 0)
    col = lax.broadcasted_iota(jnp.int32, (d, 2 * d), 1)
    eye2 = jnp.where((col == row) | (col == row + d), 1.0, 0.0).astype(BF16)

    def body(t, carry):
        lns = [slice(h * d, (h + 1) * d) for h in range(heads)]
        kp = [_bf(jnp.broadcast_to(kp_scr[t, :, 2 * h * d:(2 * h + 2) * d], (d, 2 * d))) for h in range(heads)]
        k_col = [_dot_nt(eye2, x) for x in kp]
        s_new = [s_ref[t, h] * jnp.exp(lg_ref[h]) + k_col[h] * v_scr[t, :, lns[h]] for h in range(heads)]
        for h in range(heads):
            o_ref[t, h] = s_new[h]
        y = [_dot(_bf(jnp.broadcast_to(q_scr[t, :, lns[h]], (SUBLANES, d))), _bf(s_new[h])) for h in range(heads)]
        for h in range(heads):
            y_scr[t, :, lns[h]] = y[h][0:1]
        return carry

    lax.fori_loop(0, tt, body, 0)
    for h in range(heads):
        ln = slice(h * d, (h + 1) * d)
        g = g_ref[:, ln]
        yn = _head_norm(y_scr[:, 0, ln], RET_LN_EPS) * lnw_ref[:, ln]
        y_ref[:, ln] = (g * jax.nn.sigmoid(g) * yn).astype(y_ref.dtype)


def _in_place(bufs, n_inputs):
    specs, args, aliases = [], [], {}
    for out_idx, buf in enumerate(bufs):
        if buf is not None:
            aliases[n_inputs + len(args)] = out_idx
            specs.append(pl.BlockSpec(memory_space=pl.ANY))
            args.append(buf)
    return specs, args, aliases


def _ret_sample(proj, states, layer, y_buf, new_states, cos, sin, lg, ln_w, *, row_start, width):
    bs, heads = states.shape[1], states.shape[2]
    tt = STEP_TOKENS
    assert row_start % tt == 0 and bs % tt == 0
    roff = row_start // tt

    def col(which):
        return pl.BlockSpec((tt, width), lambda i: (i + roff, which))

    st = pl.BlockSpec((None, tt, heads, RET_HEAD_DIM, RET_HEAD_DIM), lambda i: (layer, i, 0, 0, 0))
    in_specs = [col(0), col(1), col(2), col(3), st,
                pl.BlockSpec((1, RET_HEAD_DIM), lambda i: (0, 0)),
                pl.BlockSpec((1, RET_HEAD_DIM), lambda i: (0, 0)),
                pl.BlockSpec((heads, 1, LANES), lambda i: (0, 0, 0)),
                pl.BlockSpec((1, width), lambda i: (0, 0))]
    alias_specs, alias_args, aliases = _in_place([y_buf, new_states], len(in_specs))
    return pl.pallas_call(
        _ret_sample_kernel,
        grid=(bs // tt,),
        in_specs=in_specs + alias_specs,
        out_specs=[pl.BlockSpec((tt, width), lambda i: (i + roff, 0)), st],
        out_shape=[jax.ShapeDtypeStruct(y_buf.shape, y_buf.dtype), jax.ShapeDtypeStruct(states.shape, states.dtype)],
        scratch_shapes=[pltpu.VMEM((tt, 1, width), F32), pltpu.VMEM((tt, 1, 2 * width), F32),
                        pltpu.VMEM((tt, 1, width), F32), pltpu.VMEM((tt, 1, width), F32)],
        input_output_aliases=aliases,
        compiler_params=_params("parallel"),
        name="ret_sample",
    )(proj, proj, proj, proj, states, cos, sin, lg, ln_w, *alias_args)


def _softplus(x):
    return jnp.maximum(x, 0.0) + jnp.log1p(jnp.exp(-jnp.abs(x)))


def _group_matrix():
    r = lax.broadcasted_iota(jnp.int32, (LANES, LANES), 0) // RWKV_HEAD_DIM
    c = lax.broadcasted_iota(jnp.int32, (LANES, LANES), 1) // RWKV_HEAD_DIM
    return jnp.where(r == c, 1.0, 0.0).astype(BF16)


def _group_sum(x, gmat):
    outs = []
    for j in range(x.shape[1] // LANES):
        hi, lo = _split2(x[:, j * LANES:(j + 1) * LANES])
        outs.append(_dot(hi, gmat) + _dot(lo, gmat))
    return outs[0] if len(outs) == 1 else jnp.concatenate(outs, axis=1)


def _cumsum_rows(x, tri):
    hi = _bf(x)
    r1 = x - hi.astype(F32)
    mid = _bf(r1)
    lo = _bf(r1 - mid.astype(F32))
    return _dot(tri, hi) + _dot(tri, mid) + _dot(tri, lo)


def _rwkv_prep_kernel(*refs, seq_mode, has_vres, tiles_per_seq, width):
    it = iter(refs)
    cur = [next(it) for _ in range(4)]
    prev = [next(it) for _ in range(4)]
    mu_ref, w0_ref, wup_ref, a0_ref, aup_ref, gup_ref, kk_ref, ka_ref, rk_ref = (next(it) for _ in range(9))
    if has_vres:
        v0_ref, vup_ref, vfirst_ref = (next(it) for _ in range(3))
    outs = list(it)
    i = pl.program_id(0)

    def shifted(idx, mu):
        p = cur[idx][...]
        if seq_mode:
            carry = jnp.where(i % tiles_per_seq == 0, 0.0, prev[idx][SUBLANES - 1:SUBLANES, :])
            rowi = lax.broadcasted_iota(jnp.int32, p.shape, 0)
            p_prev = jnp.where(rowi == 0, carry, pltpu.roll(p, 1, 0))
        else:
            p_prev = prev[idx][...]
        return p + (p_prev - p) * mu

    w = width
    r = shifted(0, mu_ref[:, 0:w])
    k = shifted(1, mu_ref[:, w:2 * w])
    v = shifted(2, mu_ref[:, 2 * w:3 * w])
    lo = shifted(3, mu_ref[:, 3 * w:])
    lob = _bf(lo)
    w_raw = -_softplus(-(w0_ref[...] + _dot(_bf(jnp.tanh(lo)), wup_ref[...]))) - 0.5
    lw = -jnp.exp(w_raw)
    a = jax.nn.sigmoid(a0_ref[...] + _dot(lob, aup_ref[...]))
    g = _dot(_bf(jax.nn.sigmoid(lo)), gup_ref[...])
    if has_vres:
        v = v + (vfirst_ref[...] - v) * jax.nn.sigmoid(v0_ref[...] + _dot(lob, vup_ref[...]))
    k2 = k * (1.0 + (a - 1.0) * ka_ref[...])
    kk = k * kk_ref[...]
    gmat = _group_matrix()
    kk = kk / jnp.maximum(jnp.sqrt(_group_sum(kk * kk, gmat)), 1e-12)
    b = kk * a
    if not seq_mode:
        for o_ref, val in zip(outs, (r, lw, k2, v, kk, b, g)):
            o_ref[...] = val
        return

    tp, c = r.shape[0], RWKV_CHUNK
    row = lax.broadcasted_iota(jnp.int32, (tp, tp), 0)
    col = lax.broadcasted_iota(jnp.int32, (tp, tp), 1)
    tri = jnp.where(row >= col, jnp.where(row // c == col // c, 1.0, 0.0), 0.0).astype(BF16)
    cum = _cumsum_rows(lw, tri)
    last = [cum[j * c + c - 1:j * c + c, :] for j in range(tp // c)]
    tot = jnp.concatenate([jnp.broadcast_to(x, (c, w)) for x in last], axis=0)
    e_in = jnp.exp(-cum)
    e_out = jnp.exp(tot - cum)
    bonus = _group_sum(r * k2 * rk_ref[...], gmat) * v
    vals = (kk * jnp.exp(cum - lw), r * jnp.exp(cum), b * e_in, k2 * e_in, b * e_out, k2 * e_out, v, bonus, g)
    for o_ref, val in zip(outs, vals):
        o_ref[...] = val.astype(o_ref.dtype)
    outs[9][...] = jnp.concatenate([jnp.broadcast_to(jnp.exp(x), (SUBLANES, w)) for x in last], axis=0)
    if not has_vres:
        outs[10][...] = v


def _rwkv_prep(proj, prev_proj, lp, vfirst, *, row_start, rows, seq, ret_cols, width):
    seq_mode = prev_proj is None
    has_vres = vfirst is not None
    tp = _pick(seq, (256, 128, 64)) if seq_mode else _pick(rows, (128, 64, 32, 16, 8))
    assert row_start % tp == 0 and ret_cols % width == 0 and (ret_cols + 3 * width) % LORA_PAD == 0
    roff = row_start // tp
    cb = ret_cols // width
    lb = (ret_cols + 3 * width) // LORA_PAD
    widths = (width, width, width, LORA_PAD)
    cur_blocks = (cb, cb + 1, cb + 2, lb)
    in_specs, args = [], []
    for wd, blk in zip(widths, cur_blocks):
        in_specs.append(pl.BlockSpec((tp, wd), lambda i, blk=blk: (i + roff, blk)))
        args.append(proj)
    if seq_mode:
        for wd, blk in zip(widths, cur_blocks):
            in_specs.append(pl.BlockSpec(
                (SUBLANES, wd), lambda i, blk=blk: (jnp.maximum((i + roff) * (tp // SUBLANES) - 1, 0), blk)))
            args.append(proj)
    else:
        for wd, blk in zip(widths, (0, 1, 2, 3 * width // LORA_PAD)):
            in_specs.append(pl.BlockSpec((tp, wd), lambda i, blk=blk: (i, blk)))
            args.append(prev_proj)

    def full(x):
        return pl.BlockSpec(x.shape, lambda i: (0, 0))

    names = ["mu", "w0", "w_up", "a0", "a_up", "g_up", "k_k", "k_a", "r_k"] + (["v0", "v_up"] if has_vres else [])
    for nm in names:
        in_specs.append(full(lp[nm]))
        args.append(lp[nm])
    if has_vres:
        in_specs.append(pl.BlockSpec((tp, width), lambda i: (i, 0)))
        args.append(vfirst)
    tok = pl.BlockSpec((tp, width), lambda i: (i, 0))
    if seq_mode:
        assert tp % RWKV_CHUNK == 0
        per_chunk = pl.BlockSpec((tp // RWKV_CHUNK * SUBLANES, width), lambda i: (i, 0))
        out_specs = [tok] * 9 + [per_chunk] + ([] if has_vres else [tok])
        out_shape = ([jax.ShapeDtypeStruct((rows, width), BF16)] * 9
                     + [jax.ShapeDtypeStruct((rows // RWKV_CHUNK * SUBLANES, width), F32)]
                     + ([] if has_vres else [jax.ShapeDtypeStruct((rows, width), F32)]))
    else:
        out_specs = [tok] * 7
        out_shape = [jax.ShapeDtypeStruct((rows, width), F32)] * 7
    return pl.pallas_call(
        functools.partial(_rwkv_prep_kernel, seq_mode=seq_mode, has_vres=has_vres,
                          tiles_per_seq=max(seq // tp, 1), width=width),
        grid=(rows // tp,),
        in_specs=in_specs,
        out_specs=out_specs,
        out_shape=out_shape,
        compiler_params=_params("parallel"),
        name="rwkv_prep",
    )(*args)


def _rwkv_scan_kernel(khat_ref, rhat_ref, btil_ref, ktil_ref, bbar_ref, kbar_ref, v_ref, bonus_ref, g_ref,
                      dtot_ref, lnw_ref, lnb_ref, y_ref, s_ref,
                      r_scr, o0_scr, m_scr, n_scr, o_scr, *, n_chunks, chunk_group, post_rows):
    c, n = RWKV_CHUNK, RWKV_HEAD_DIM
    pairs = khat_ref.shape[1] // LANES
    row = lax.broadcasted_iota(jnp.int32, (c, c), 0)
    col = lax.broadcasted_iota(jnp.int32, (c, c), 1)
    strict, incl = row > col, row >= col
    eye = jnp.where(row == col, 1.0, 0.0)
    first = lax.broadcasted_iota(jnp.int32, (c, LANES), 1) < n
    first_b = jnp.where(first, 1.0, 0.0).astype(BF16)
    second_b = jnp.where(first, 0.0, 1.0).astype(BF16)
    vrow = lax.broadcasted_iota(jnp.int32, (LANES, LANES), 0) < n
    kcol = lax.broadcasted_iota(jnp.int32, (LANES, LANES), 1) < n
    same_head = vrow == kcol

    def lanes(p):
        return slice(p * LANES, (p + 1) * LANES)

    def pass1(gi, carry):
        inst = [(gi * chunk_group + cj, p) for cj in range(chunk_group) for p in range(pairs)]
        rows = [pl.ds(pl.multiple_of(ci * c, c), c) for ci, _ in inst]
        at = [(rows[i], lanes(p)) for i, (_, p) in enumerate(inst)]
        khat = [khat_ref[a] for a in at]
        rhat = [rhat_ref[a] for a in at]
        v2 = [v_ref[a] for a in at]
        x = [jnp.concatenate([kh * first_b, kh * second_b, rh * first_b, rh * second_b], axis=0)
             for kh, rh in zip(khat, rhat)]
        ab = [_dot_nt(xi, btil_ref[a]) for xi, a in zip(x, at)]
        ak = [_dot_nt(xi, ktil_ref[a]) for xi, a in zip(x, at)]
        heads2 = [(i, hh) for i in range(len(inst)) for hh in range(2)]
        pw = [-jnp.where(strict, ab[i][hh * c:(hh + 1) * c], 0.0) for i, hh in heads2]
        a_k = [_bf(jnp.where(strict, ak[i][hh * c:(hh + 1) * c], 0.0)) for i, hh in heads2]
        a_rk = [_bf(jnp.where(incl, ak[i][(2 + hh) * c:(3 + hh) * c], 0.0)) for i, hh in heads2]
        a_rb = [_bf(jnp.where(incl, ab[i][(2 + hh) * c:(3 + hh) * c], 0.0)) for i, hh in heads2]
        akv = [_dot(a_k[j], v2[i]) for j, (i, _) in enumerate(heads2)]
        arkv = [_dot(a_rk[j], v2[i]) for j, (i, _) in enumerate(heads2)]
        akv_b = [_bf(jnp.where(first, akv[2 * i], akv[2 * i + 1])) for i in range(len(inst))]
        tinv = [eye + q for q in pw]
        for _ in range(c.bit_length() - 2):
            pwb = [_bf(q) for q in pw]
            pw = [_dot(q, q) for q in pwb]
            tinv = [t + _dot(_bf(t), _bf(q)) for t, q in zip(tinv, pw)]
        tinv = [_bf(t) for t in tinv]
        wt = [_dot(tinv[j], khat[i]) for j, (i, _) in enumerate(heads2)]
        u0 = [_dot(tinv[j], akv_b[i]) for j, (i, _) in enumerate(heads2)]
        wt_b = [_bf(jnp.where(first, wt[2 * i], wt[2 * i + 1])) for i in range(len(inst))]
        u0_b = [_bf(jnp.where(first, u0[2 * i], u0[2 * i + 1])) for i in range(len(inst))]
        bbar = [bbar_ref[a] for a in at]
        m = [_dot_tn(wt_b[i], bbar[i]) for i in range(len(inst))]
        nn = [_dot_tn(jnp.concatenate([v2[i], -u0_b[i]], axis=0),
                      jnp.concatenate([kbar_ref[at[i]], bbar[i]], axis=0)) for i in range(len(inst))]
        rw = [_dot(a_rb[j], wt_b[i]) for j, (i, _) in enumerate(heads2)]
        ru = [_dot(a_rb[j], u0_b[i]) for j, (i, _) in enumerate(heads2)]
        for i, (ci, p) in enumerate(inst):
            m_scr[ci, p] = _bf(jnp.where(same_head, -m[i], 0.0))
            n_scr[ci, p] = jnp.where(same_head, nn[i], 0.0)
            r_scr[at[i]] = _bf(rhat[i].astype(F32) - jnp.where(first, rw[2 * i], rw[2 * i + 1]))
            o0_scr[at[i]] = jnp.where(first, arkv[2 * i] - ru[2 * i], arkv[2 * i + 1] - ru[2 * i + 1])
        return carry

    lax.fori_loop(0, n_chunks // chunk_group, pass1, 0)

    def pass2(ci, states):
        rows = pl.ds(pl.multiple_of(ci * c, c), c)
        drows = pl.ds(pl.multiple_of(ci * SUBLANES, SUBLANES), SUBLANES)
        sb = [_bf(s) for s in states]
        ds = [_dot(sb[p], m_scr[ci, p]) for p in range(pairs)]
        for p in range(pairs):
            at = (rows, lanes(p))
            o_scr[at] = _dot_nt(r_scr[at], sb[p]) + o0_scr[at]
        return tuple(states[p] * dtot_ref[drows, lanes(p)][0:1] + ds[p] + n_scr[ci, p] for p in range(pairs))

    final = lax.fori_loop(0, n_chunks, pass2, tuple(jnp.zeros((LANES, LANES), F32) for _ in range(pairs)))
    for p in range(pairs):
        s_ref[0, 2 * p] = final[p][:n, :n]
        s_ref[0, 2 * p + 1] = final[p][n:, n:]

    gmat = _group_matrix()

    def post(ti, carry):
        rows = pl.ds(pl.multiple_of(ti * post_rows, post_rows), post_rows)
        o = o_scr[rows, :]
        oc = o - _group_sum(o, gmat) * (1.0 / n)
        var = _group_sum(oc * oc, gmat) * (1.0 / n)
        y = oc * lax.rsqrt(var + RWKV_LN_EPS) * lnw_ref[...] + lnb_ref[...]
        y_ref[rows, :] = ((y + bonus_ref[rows, :].astype(F32)) * g_ref[rows, :].astype(F32)).astype(y_ref.dtype)
        return carry

    lax.fori_loop(0, (n_chunks * c) // post_rows, post, 0)


def _rwkv_scan(feats, lnw, lnb, *, batch, seq, width, total_rows):
    heads = width // RWKV_HEAD_DIM
    blk = _pick(width, (2 * LANES, LANES))
    hb = blk // RWKV_HEAD_DIM
    c = RWKV_CHUNK
    assert seq % c == 0
    n_chunks = seq // c
    tok = pl.BlockSpec((seq, blk), lambda b, h: (b, h))
    par = pl.BlockSpec((1, blk), lambda b, h: (0, h))
    return pl.pallas_call(
        functools.partial(_rwkv_scan_kernel, n_chunks=n_chunks, chunk_group=_pick(n_chunks, (4, 2, 1)),
                          post_rows=_pick(seq, (256, 128, 64))),
        grid=(batch, width // blk),
        in_specs=[tok] * 9 + [pl.BlockSpec((n_chunks * SUBLANES, blk), lambda b, h: (b, h)), par, par],
        out_specs=[tok, pl.BlockSpec((1, hb, RWKV_HEAD_DIM, RWKV_HEAD_DIM), lambda b, h: (b, h, 0, 0))],
        out_shape=[jax.ShapeDtypeStruct((total_rows, width), BF16),
                   jax.ShapeDtypeStruct((batch, heads, RWKV_HEAD_DIM, RWKV_HEAD_DIM), F32)],
        scratch_shapes=[pltpu.VMEM((seq, blk), BF16), pltpu.VMEM((seq, blk), F32),
                        pltpu.VMEM((n_chunks, blk // LANES, LANES, LANES), BF16),
                        pltpu.VMEM((n_chunks, blk // LANES, LANES, LANES), F32),
                        pltpu.VMEM((seq, blk), F32)],
        compiler_params=_params("parallel", "parallel"),
        name="rwkv_scan",
    )(*feats[:10], lnw, lnb)


def _rwkv_sample_kernel(r_ref, lw_ref, k_ref, v_ref, kk_ref, b_ref, g_ref, s_ref, rk_ref, lnw_ref, lnb_ref, *rest):
    y_ref, o_ref, v_scr, out_scr = rest[-4:]
    n = RWKV_HEAD_DIM
    tokens = r_ref.shape[0]

    def chan(ref):
        return ref[...].T

    def col(ref):
        return jnp.broadcast_to(ref[...], (tokens, LANES)).T

    r, k2, v, kk, b, g = chan(r_ref), chan(k_ref), chan(v_ref), chan(kk_ref), chan(b_ref), chan(g_ref)
    w = jnp.exp(chan(lw_ref))
    v_scr[:, 0, :] = v
    for hh in range(s_ref.shape[0]):
        ch = slice(hh * n, (hh + 1) * n)
        kk_h, w_h, b_h, k_h, r_h = kk[ch], w[ch], b[ch], k2[ch], r[ch]

        def body(vi, carry):
            s = s_ref[hh, vi]
            sa = -jnp.sum(s * kk_h, axis=0, keepdims=True)
            new = s * w_h + sa * b_h + v_scr[hh * n + vi] * k_h
            o_ref[hh, vi] = new
            out_scr[hh * n + vi] = jnp.sum(new * r_h, axis=0, keepdims=True)
            return carry

        lax.fori_loop(0, n, body, 0, unroll=4)
    o = out_scr[:, 0, :]
    rkk = r * k2 * col(rk_ref)
    yn, bonus = [], []
    for hh in range(s_ref.shape[0]):
        ch = slice(hh * n, (hh + 1) * n)
        oc = o[ch] - jnp.mean(o[ch], axis=0, keepdims=True)
        var = jnp.mean(oc * oc, axis=0, keepdims=True)
        yn.append(oc * lax.rsqrt(var + RWKV_LN_EPS))
        bonus.append(jnp.sum(rkk[ch], axis=0, keepdims=True) * v[ch])
    y = (jnp.concatenate(yn, axis=0) * col(lnw_ref) + col(lnb_ref) + jnp.concatenate(bonus, axis=0)) * g
    y_ref[...] = y.T.astype(y_ref.dtype)


def _rwkv_sample(feats, states_t, layer, y_buf, new_states, rk, lnw, lnb, *, row_start, width):
    heads, bs = states_t.shape[1], states_t.shape[4]
    hb = LANES // RWKV_HEAD_DIM
    assert row_start % bs == 0 and heads % hb == 0
    roff = row_start // bs
    tok = pl.BlockSpec((bs, LANES), lambda j: (0, j))
    par = pl.BlockSpec((1, LANES), lambda j: (0, j))
    st = pl.BlockSpec((None, hb, RWKV_HEAD_DIM, RWKV_HEAD_DIM, bs), lambda j: (layer, j, 0, 0, 0))
    in_specs = [tok] * 7 + [st] + [par] * 3
    alias_specs, alias_args, aliases = _in_place([y_buf, new_states], len(in_specs))
    return pl.pallas_call(
        _rwkv_sample_kernel,
        grid=(heads // hb,),
        in_specs=in_specs + alias_specs,
        out_specs=[pl.BlockSpec((bs, LANES), lambda j: (roff, j)), st],
        out_shape=[jax.ShapeDtypeStruct(y_buf.shape, y_buf.dtype),
                   jax.ShapeDtypeStruct(states_t.shape, states_t.dtype)],
        scratch_shapes=[pltpu.VMEM((LANES, 1, bs), F32), pltpu.VMEM((LANES, 1, bs), F32)],
        input_output_aliases=aliases,
        compiler_params=_params("parallel"),
        name="rwkv_sample",
    )(*feats, states_t, rk, lnw, lnb, *alias_args)


def _rope_tables(pos):
    half = RET_HEAD_DIM // 2
    inv_freq = ROPE_BASE ** (-jnp.arange(half, dtype=F32) / half)
    ang = pos[:, None] * inv_freq[None, :]
    cos, sin = jnp.cos(ang), jnp.sin(ang)
    return jnp.concatenate([cos, cos], axis=-1), jnp.concatenate([-sin, sin], axis=-1)


def _pad_rows(w, start, total):
    return jnp.zeros((total, w.shape[1]), BF16).at[start:start + w.shape[0]].set(_bf(w))


def kernel(x_prompt, x_sample, state_ret, state_rwkv, state_shift, w_in, w_in_vres, mu_shift, mu_shift_vres, ret_ln_w, rwkv_w0, rwkv_w_up, rwkv_a0, rwkv_a_up, rwkv_g_up, rwkv_v0, rwkv_v_up, rwkv_k_k, rwkv_k_a, rwkv_r_k, rwkv_ln_w, rwkv_ln_b, w_out, g_attn, g_ffn, w_gate, w_up, w_down, g_final):
    batch, seq, d = x_prompt.shape
    bs = x_sample.shape[0]
    assert x_sample.shape[1] == 1
    depth = w_in.shape[0]
    mp = batch * seq
    ret_w = d // 2
    rw_w = d - ret_w
    ret_cols = 4 * ret_w
    main_cols = ret_cols + 3 * rw_w
    r_decay, r_aaa, r_gate = rwkv_w_up.shape[1], rwkv_a_up.shape[1], rwkv_g_up.shape[1]
    r_mv = rwkv_v_up.shape[1]
    o_a, o_g = r_decay, r_decay + r_aaa
    o_v = o_g + r_gate
    assert o_v + r_mv <= LORA_PAD and w_in.shape[2] == main_cols + o_v
    ret_heads = ret_w // RET_HEAD_DIM

    log_gamma = jnp.log1p(-jnp.exp2(-5.0 - jnp.arange(ret_heads, dtype=F32)))
    lg = jnp.broadcast_to(log_gamma[:, None, None], (ret_heads, 1, LANES))
    cos_p, sin_p = _rope_tables(jnp.arange(seq, dtype=F32))
    cos_s, sin_s = _rope_tables(PAST_LEN + jnp.arange(1, dtype=F32))

    pad = LORA_PAD - o_v - r_mv
    w_mix = _bf(jnp.concatenate(
        [w_in, jnp.concatenate([jnp.zeros((1, d, r_mv), F32), w_in_vres], axis=0), jnp.zeros((depth, d, pad), F32)],
        axis=-1))
    mu_all = jnp.concatenate(
        [mu_shift, jnp.concatenate([jnp.zeros((1, r_mv), F32), mu_shift_vres], axis=0), jnp.zeros((depth, pad), F32)],
        axis=-1)
    w_o, w_g, w_u, w_d = _bf(w_out), _bf(w_gate), _bf(w_up), _bf(w_down)
    state_rwkv_t = jnp.transpose(state_rwkv, (0, 2, 3, 4, 1))

    h = jnp.concatenate([x_prompt.reshape(mp, d), x_sample.reshape(bs, d)], axis=0)
    vfirst_p = vfirst_s = None
    ret_p, rwkv_p, shift_p, shift_s = [], [], [], []
    ret_s = rwkv_s = None
    for l in range(depth):
        lp = {
            "mu": mu_all[l].reshape(1, -1),
            "w0": rwkv_w0[l].reshape(1, -1), "w_up": _pad_rows(rwkv_w_up[l], 0, LORA_PAD),
            "a0": rwkv_a0[l].reshape(1, -1), "a_up": _pad_rows(rwkv_a_up[l], o_a, LORA_PAD),
            "g_up": _pad_rows(rwkv_g_up[l], o_g, LORA_PAD),
            "k_k": rwkv_k_k[l].reshape(1, -1), "k_a": rwkv_k_a[l].reshape(1, -1),
            "r_k": rwkv_r_k[l].reshape(1, -1),
        }
        if l > 0:
            lp["v0"] = rwkv_v0[l - 1].reshape(1, -1)
            lp["v_up"] = _pad_rows(rwkv_v_up[l - 1], o_v, LORA_PAD)
        lnw, lnb = rwkv_ln_w[l].reshape(1, -1), rwkv_ln_b[l].reshape(1, -1)
        ret_lnw = ret_ln_w[l].reshape(1, -1)

        xn = _rmsnorm(h, g_attn[l], BF16)
        proj = _matmul([(xn, w_mix, 0)], l, main_cols + LORA_PAD, F32, name="in_proj")
        h_last = jnp.concatenate([h[seq - 1:mp:seq], h[mp:]], axis=0)
        xn_last = _rmsnorm(h_last, g_attn[l], F32)
        shift_p.append(xn_last[:batch])
        shift_s.append(xn_last[batch:])
        prev_s = _matmul([(_bf(state_shift[l]), w_mix, 0)], l, 3 * rw_w + LORA_PAD, F32,
                         col_start=ret_cols, name="prev_proj")

        feats_p = _rwkv_prep(proj, None, lp, vfirst_p, row_start=0, rows=mp, seq=seq,
                             ret_cols=ret_cols, width=rw_w)
        feats_s = _rwkv_prep(proj, prev_s, lp, vfirst_s, row_start=mp, rows=bs, seq=1,
                             ret_cols=ret_cols, width=rw_w)
        if l == 0:
            vfirst_p, vfirst_s = feats_p[10], feats_s[3]

        y_ret, s_ret_p = _ret_prompt(proj, cos_p, sin_p, lg, ret_lnw, batch=batch, seq=seq, width=ret_w,
                                     total_rows=mp + bs)
        y_ret, ret_s = _ret_sample(proj, state_ret, l, y_ret, ret_s, cos_s, sin_s, lg, ret_lnw,
                                   row_start=mp, width=ret_w)
        y_rw, s_rw_p = _rwkv_scan(feats_p, lnw, lnb, batch=batch, seq=seq, width=rw_w, total_rows=mp + bs)
        y_rw, rwkv_s = _rwkv_sample(feats_s, state_rwkv_t, l, y_rw, rwkv_s, lp["r_k"], lnw, lnb,
                                    row_start=mp, width=rw_w)
        ret_p.append(s_ret_p)
        rwkv_p.append(s_rw_p)

        h = _matmul([(y_ret, w_o, 0), (y_rw, w_o, ret_w)], l, d, F32, res=h, name="out_proj")
        hn = _rmsnorm(h, g_ffn[l], BF16)
        act = _swiglu(hn, w_g, w_u, l)
        h = _matmul([(act, w_d, 0)], l, d, F32, res=h, name="ffn_down")

    y_p = _rmsnorm(h, g_final, F32, row_start=0, rows=mp).reshape(batch, seq, d)
    y_s = _rmsnorm(h, g_final, F32, row_start=mp, rows=bs, tm=_pick(bs, (128, 64, 32, 16, 8))).reshape(bs, 1, d)
    return (y_p, y_s, jnp.stack(ret_p), jnp.stack(rwkv_p), jnp.stack(shift_p),
            ret_s, jnp.transpose(rwkv_s, (0, 4, 1, 2, 3)), jnp.stack(shift_s))
```

```python
import functools

import jax
import jax.numpy as jnp
from jax import lax
from jax.experimental import pallas as pl
from jax.experimental.pallas import tpu as pltpu

F32 = jnp.float32
BF16 = jnp.bfloat16

PAST_LEN = 16384
ROPE_BASE = 10000.0
RET_HEAD_DIM = 128
RET_CHUNK = 128
RWKV_HEAD_DIM = 64
RWKV_CHUNK = 64
NORM_EPS = 1e-6
RET_LN_EPS = 1e-5
RWKV_LN_EPS = 64e-5

LANES = 128
SUBLANES = 8
LORA_PAD = 512
VMEM_LIMIT_BYTES = 56 * 1024 * 1024
ROW_TILES = (1664, 832, 640, 512, 256, 128, 64, 32, 16)
FFN_DOWN_ROW_TILES = ROW_TILES[2:]
STEP_TOKENS = 8


def _params(*sem):
    return pltpu.CompilerParams(dimension_semantics=sem, vmem_limit_bytes=VMEM_LIMIT_BYTES)


def _pick(n, candidates):
    for c in candidates:
        if n % c == 0:
            return c
    raise ValueError(f"no tile in {candidates} divides {n}")


def _dot(a, b):
    return jnp.dot(a, b, preferred_element_type=F32)


def _dot_nt(a, b):
    return lax.dot_general(a, b, (((1,), (1,)), ((), ())), preferred_element_type=F32)


def _dot_tn(a, b):
    return lax.dot_general(a, b, (((0,), (0,)), ((), ())), preferred_element_type=F32)


def _bf(x):
    return x.astype(BF16)


def _split2(x):
    hi = _bf(x)
    return hi, _bf(x - hi.astype(F32))


def _rmsnorm_kernel(x_ref, g_ref, o_ref):
    x = x_ref[...]
    y = x * lax.rsqrt(jnp.mean(x * x, axis=-1, keepdims=True) + NORM_EPS)
    o_ref[...] = (y * g_ref[...]).astype(o_ref.dtype)


def _rmsnorm(x, g, out_dtype, *, row_start=0, rows=None, tm=None):
    m, d = x.shape
    rows = m if rows is None else rows
    if tm is None:
        tm = rows if rows <= 640 else _pick(rows, (640, 512, 256, 128, 64, 32, 16))
    assert row_start % tm == 0 and rows % tm == 0
    off = row_start // tm
    return pl.pallas_call(
        _rmsnorm_kernel,
        grid=(rows // tm,),
        in_specs=[pl.BlockSpec((tm, d), lambda i: (i + off, 0)), pl.BlockSpec((1, d), lambda i: (0, 0))],
        out_specs=pl.BlockSpec((tm, d), lambda i: (i, 0)),
        out_shape=jax.ShapeDtypeStruct((rows, d), out_dtype),
        compiler_params=_params("parallel"),
        name="rmsnorm",
    )(x, g.reshape(1, d))


def _mm_kernel(*refs, n_pairs, has_res, transposed, swiglu):
    n_w = 2 * n_pairs if swiglu else n_pairs
    a_refs, w_refs = refs[:n_pairs], refs[n_pairs:n_pairs + n_w]
    w_scr = refs[len(refs) - n_w:]
    o_ref = refs[len(refs) - n_w - 1]

    @pl.when(pl.program_id(1) == 0)
    def _():
        for w_ref, scr in zip(w_refs, w_scr):
            scr[...] = w_ref[...].astype(BF16)

    def product(scrs):
        acc = None
        for a_ref, scr in zip(a_refs, scrs):
            d = _dot_nt(a_ref[...], scr[...]) if transposed else _dot(a_ref[...], scr[...])
            acc = d if acc is None else acc + d
        return acc

    acc = product(w_scr[:n_pairs])
    if swiglu:
        acc = acc * jax.nn.sigmoid(acc) * product(w_scr[n_pairs:])
    if has_res:
        acc = acc + refs[n_pairs + n_w][...]
    o_ref[...] = acc.astype(o_ref.dtype)


def _matmul(pairs, layer, n, out_dtype, *, res=None, col_start=0, transposed=False, gate=None, tm=None, tn=512,
            name="matmul"):
    m = pairs[0][0].shape[0]
    tm = _pick(m, ROW_TILES) if tm is None else tm
    assert col_start % tn == 0 and n % tn == 0 and m % tm == 0
    coff = col_start // tn
    in_specs, args, scratch = [], [], []
    for a, _, _ in pairs:
        in_specs.append(pl.BlockSpec((tm, a.shape[1]), lambda j, i: (i, 0)))
        args.append(a)
    weights = [(a.shape[1], g, k_start) for (a, _, k_start), g in zip(pairs, gate or [])]
    weights += [(a.shape[1], w, k_start) for a, w, k_start in pairs]
    for k, w, k_start in weights:
        assert k_start % k == 0
        kb = k_start // k
        if transposed:
            in_specs.append(pl.BlockSpec((None, tn, k), lambda j, i, kb=kb: (layer, j + coff, kb)))
            scratch.append(pltpu.VMEM((tn, k), BF16))
        else:
            in_specs.append(pl.BlockSpec((None, k, tn), lambda j, i, kb=kb: (layer, kb, j + coff)))
            scratch.append(pltpu.VMEM((k, tn), BF16))
        args.append(w)
    if res is not None:
        in_specs.append(pl.BlockSpec((tm, tn), lambda j, i: (i, j)))
        args.append(res)
    return pl.pallas_call(
        functools.partial(_mm_kernel, n_pairs=len(pairs), has_res=res is not None, transposed=transposed,
                          swiglu=gate is not None),
        grid=(n // tn, m // tm),
        in_specs=in_specs,
        out_specs=pl.BlockSpec((tm, tn), lambda j, i: (i, j)),
        out_shape=jax.ShapeDtypeStruct((m, n), out_dtype),
        scratch_shapes=scratch,
        compiler_params=_params("parallel", "arbitrary"),
        name=name,
    )(*args)


def _rope(x, cos, sin_signed):
    return x * cos + pltpu.roll(x, RET_HEAD_DIM // 2, 1) * sin_signed


def _head_norm(y, eps):
    mu = jnp.mean(y, axis=-1, keepdims=True)
    yc = y - mu
    var = jnp.mean(yc * yc, axis=-1, keepdims=True)
    return yc * lax.rsqrt(var + eps)


def _ret_prompt_kernel(q_ref, k_ref, v_ref, g_ref, cos_ref, sin_ref, lg_ref, lnw_ref, y_ref, s_ref, *,
                       n_chunks, group):
    c, d = RET_CHUNK, RET_HEAD_DIM
    lg = lg_ref[0]
    row = lax.broadcasted_iota(jnp.int32, (c, c), 0)
    col = lax.broadcasted_iota(jnp.int32, (c, c), 1)
    diff = (row - col).astype(F32)
    decay_in = jnp.where(diff >= 0, jnp.exp(lg * jnp.maximum(diff, 0.0)), 0.0)
    ridx = lax.broadcasted_iota(jnp.int32, (c, d), 0).astype(F32)
    tail = jnp.exp((c - 1 - ridx) * lg)
    head = jnp.exp((ridx + 1) * lg)
    chunk_decay = jnp.exp(c * lg)
    lnw = lnw_ref[...]

    def body(gi, s):
        rows = [pl.ds(pl.multiple_of((gi * group + j) * c, c), c) for j in range(group)]
        cs = [(cos_ref[r, :], sin_ref[r, :]) for r in rows]
        q = [_rope(q_ref[r, :], co, si) for r, (co, si) in zip(rows, cs)]
        k = [_rope(k_ref[r, :], co, si) * (d ** -0.5) for r, (co, si) in zip(rows, cs)]
        vb = [_bf(v_ref[r, :]) for r in rows]
        scores = [_dot_nt(_bf(qj), _bf(kj)) * decay_in for qj, kj in zip(q, k)]
        kv = [_dot_tn(_bf(kj * tail), vj) for kj, vj in zip(k, vb)]
        states = [s]
        for j in range(group):
            states.append(states[j] * chunk_decay + kv[j])
        y = [_dot(_bf(sc), vj) + _dot(_bf(qj * head), _bf(sj))
             for sc, vj, qj, sj in zip(scores, vb, q, states)]
        for r, yj in zip(rows, y):
            g = g_ref[r, :]
            y_ref[r, :] = (g * jax.nn.sigmoid(g) * (_head_norm(yj, RET_LN_EPS) * lnw)).astype(y_ref.dtype)
        return states[group]

    s_ref[0, 0] = lax.fori_loop(0, n_chunks // group, body, jnp.zeros((d, d), F32))


def _ret_prompt(proj, cos, sin, lg, ln_w, *, batch, seq, width, total_rows):
    heads = width // RET_HEAD_DIM
    assert seq % RET_CHUNK == 0 and RET_CHUNK == LANES
    n_chunks = seq // RET_CHUNK

    def col(which):
        return pl.BlockSpec((seq, RET_HEAD_DIM), lambda b, h: (b, which * heads + h))

    return pl.pallas_call(
        functools.partial(_ret_prompt_kernel, n_chunks=n_chunks, group=_pick(n_chunks, (4, 2, 1))),
        grid=(batch, heads),
        in_specs=[col(0), col(1), col(2), col(3),
                  pl.BlockSpec((seq, RET_HEAD_DIM), lambda b, h: (0, 0)),
                  pl.BlockSpec((seq, RET_HEAD_DIM), lambda b, h: (0, 0)),
                  pl.BlockSpec((1, 1, LANES), lambda b, h: (h, 0, 0)),
                  pl.BlockSpec((1, RET_HEAD_DIM), lambda b, h: (0, h))],
        out_specs=[pl.BlockSpec((seq, RET_HEAD_DIM), lambda b, h: (b, h)),
                   pl.BlockSpec((1, 1, RET_HEAD_DIM, RET_HEAD_DIM), lambda b, h: (b, h, 0, 0))],
        out_shape=[jax.ShapeDtypeStruct((total_rows, width), BF16),
                   jax.ShapeDtypeStruct((batch, heads, RET_HEAD_DIM, RET_HEAD_DIM), F32)],
        compiler_params=_params("parallel", "parallel"),
        name="ret_prompt",
    )(proj, proj, proj, proj, cos, sin, lg, ln_w)


def _ret_sample_kernel(q_ref, k_ref, v_ref, g_ref, s_ref, cos_ref, sin_ref, lg_ref, lnw_ref, *rest):
    y_ref, o_ref, q_scr, kp_scr, v_scr, y_scr = rest[-6:]
    d = RET_HEAD_DIM
    tt, width = q_ref.shape
    heads = width // d
    cos, sin = cos_ref[...], sin_ref[...]
    v_scr[:, 0, :] = v_ref[...]
    for h in range(heads):
        ln = slice(h * d, (h + 1) * d)
        q_scr[:, 0, ln] = _rope(q_ref[:, ln], cos, sin)
        k = _rope(k_ref[:, ln], cos, sin) * (d ** -0.5)
        k_hi = _bf(k).astype(F32)
        kp_scr[:, 0, 2 * h * d:(2 * h + 1) * d] = k_hi
        kp_scr[:, 0, (2 * h + 1) * d:(2 * h + 2) * d] = k - k_hi
    row = lax.broadcasted_iota(jnp.int32, (d, 2 * d), 0)
    col = lax.broadcasted_iota(jnp.int32, (d, 2 * d), 1)
    eye2 = jnp.where((col == row) | (col == row + d), 1.0, 0.0).astype(BF16)

    def body(t, carry):
        lns = [slice(h * d, (h + 1) * d) for h in range(heads)]
        kp = [_bf(jnp.broadcast_to(kp_scr[t, :, 2 * h * d:(2 * h + 2) * d], (d, 2 * d))) for h in range(heads)]
        k_col = [_dot_nt(eye2, x) for x in kp]
        s_new = [s_ref[t, h] * jnp.exp(lg_ref[h]) + k_col[h] * v_scr[t, :, lns[h]] for h in range(heads)]
        for h in range(heads):
            o_ref[t, h] = s_new[h]
        y = [_dot(_bf(jnp.broadcast_to(q_scr[t, :, lns[h]], (SUBLANES, d))), _bf(s_new[h])) for h in range(heads)]
        for h in range(heads):
            y_scr[t, :, lns[h]] = y[h][0:1]
        return carry

    lax.fori_loop(0, tt, body, 0)
    for h in range(heads):
        ln = slice(h * d, (h + 1) * d)
        g = g_ref[:, ln]
        yn = _head_norm(y_scr[:, 0, ln], RET_LN_EPS) * lnw_ref[:, ln]
        y_ref[:, ln] = (g * jax.nn.sigmoid(g) * yn).astype(y_ref.dtype)


def _in_place(bufs, n_inputs):
    specs, args, aliases = [], [], {}
    for out_idx, buf in enumerate(bufs):
        if buf is not None:
            aliases[n_inputs + len(args)] = out_idx
            specs.append(pl.BlockSpec(memory_space=pl.ANY))
            args.append(buf)
    return specs, args, aliases


def _ret_sample(proj, states, layer, y_buf, new_states, cos, sin, lg, ln_w, *, row_start, width):
    bs, heads = states.shape[1], states.shape[2]
    tt = STEP_TOKENS
    assert row_start % tt == 0 and bs % tt == 0
    roff = row_start // tt

    def col(which):
        return pl.BlockSpec((tt, width), lambda i: (i + roff, which))

    st = pl.BlockSpec((None, tt, heads, RET_HEAD_DIM, RET_HEAD_DIM), lambda i: (layer, i, 0, 0, 0))
    in_specs = [col(0), col(1), col(2), col(3), st,
                pl.BlockSpec((1, RET_HEAD_DIM), lambda i: (0, 0)),
                pl.BlockSpec((1, RET_HEAD_DIM), lambda i: (0, 0)),
                pl.BlockSpec((heads, 1, LANES), lambda i: (0, 0, 0)),
                pl.BlockSpec((1, width), lambda i: (0, 0))]
    alias_specs, alias_args, aliases = _in_place([y_buf, new_states], len(in_specs))
    return pl.pallas_call(
        _ret_sample_kernel,
        grid=(bs // tt,),
        in_specs=in_specs + alias_specs,
        out_specs=[pl.BlockSpec((tt, width), lambda i: (i + roff, 0)), st],
        out_shape=[jax.ShapeDtypeStruct(y_buf.shape, y_buf.dtype), jax.ShapeDtypeStruct(states.shape, states.dtype)],
        scratch_shapes=[pltpu.VMEM((tt, 1, width), F32), pltpu.VMEM((tt, 1, 2 * width), F32),
                        pltpu.VMEM((tt, 1, width), F32), pltpu.VMEM((tt, 1, width), F32)],
        input_output_aliases=aliases,
        compiler_params=_params("parallel"),
        name="ret_sample",
    )(proj, proj, proj, proj, states, cos, sin, lg, ln_w, *alias_args)


def _softplus(x):
    return jnp.maximum(x, 0.0) + jnp.log1p(jnp.exp(-jnp.abs(x)))


def _group_matrix():
    r = lax.broadcasted_iota(jnp.int32, (LANES, LANES), 0) // RWKV_HEAD_DIM
    c = lax.broadcasted_iota(jnp.int32, (LANES, LANES), 1) // RWKV_HEAD_DIM
    return jnp.where(r == c, 1.0, 0.0).astype(BF16)


def _group_sum(x, gmat):
    outs = []
    for j in range(x.shape[1] // LANES):
        hi, lo = _split2(x[:, j * LANES:(j + 1) * LANES])
        outs.append(_dot(hi, gmat) + _dot(lo, gmat))
    return outs[0] if len(outs) == 1 else jnp.concatenate(outs, axis=1)


def _cumsum_rows(x, tri):
    hi = _bf(x)
    r1 = x - hi.astype(F32)
    mid = _bf(r1)
    lo = _bf(r1 - mid.astype(F32))
    return _dot(tri, hi) + _dot(tri, mid) + _dot(tri, lo)


def _rwkv_prep_kernel(*refs, seq_mode, has_vres, tiles_per_seq, width):
    it = iter(refs)
    cur = [next(it) for _ in range(4)]
    prev = [next(it) for _ in range(4)]
    mu_ref, w0_ref, wup_ref, a0_ref, aup_ref, gup_ref, kk_ref, ka_ref, rk_ref = (next(it) for _ in range(9))
    if has_vres:
        v0_ref, vup_ref, vfirst_ref = (next(it) for _ in range(3))
    outs = list(it)
    i = pl.program_id(0)

    def shifted(idx, mu):
        p = cur[idx][...]
        if seq_mode:
            carry = jnp.where(i % tiles_per_seq == 0, 0.0, prev[idx][SUBLANES - 1:SUBLANES, :])
            rowi = lax.broadcasted_iota(jnp.int32, p.shape, 0)
            p_prev = jnp.where(rowi == 0, carry, pltpu.roll(p, 1, 0))
        else:
            p_prev = prev[idx][...]
        return p + (p_prev - p) * mu

    w = width
    r = shifted(0, mu_ref[:, 0:w])
    k = shifted(1, mu_ref[:, w:2 * w])
    v = shifted(2, mu_ref[:, 2 * w:3 * w])
    lo = shifted(3, mu_ref[:, 3 * w:])
    lob = _bf(lo)
    w_raw = -_softplus(-(w0_ref[...] + _dot(_bf(jnp.tanh(lo)), wup_ref[...]))) - 0.5
    lw = -jnp.exp(w_raw)
    a = jax.nn.sigmoid(a0_ref[...] + _dot(lob, aup_ref[...]))
    g = _dot(_bf(jax.nn.sigmoid(lo)), gup_ref[...])
    if has_vres:
        v = v + (vfirst_ref[...] - v) * jax.nn.sigmoid(v0_ref[...] + _dot(lob, vup_ref[...]))
    k2 = k * (1.0 + (a - 1.0) * ka_ref[...])
    kk = k * kk_ref[...]
    gmat = _group_matrix()
    kk = kk / jnp.maximum(jnp.sqrt(_group_sum(kk * kk, gmat)), 1e-12)
    b = kk * a
    if not seq_mode:
        for o_ref, val in zip(outs, (r, lw, k2, v, kk, b, g)):
            o_ref[...] = val
        return

    tp, c = r.shape[0], RWKV_CHUNK
    row = lax.broadcasted_iota(jnp.int32, (tp, tp), 0)
    col = lax.broadcasted_iota(jnp.int32, (tp, tp), 1)
    tri = jnp.where(row >= col, jnp.where(row // c == col // c, 1.0, 0.0), 0.0).astype(BF16)
    cum = _cumsum_rows(lw, tri)
    last = [cum[j * c + c - 1:j * c + c, :] for j in range(tp // c)]
    tot = jnp.concatenate([jnp.broadcast_to(x, (c, w)) for x in last], axis=0)
    e_in = jnp.exp(-cum)
    e_out = jnp.exp(tot - cum)
    bonus = _group_sum(r * k2 * rk_ref[...], gmat) * v
    vals = (kk * jnp.exp(cum - lw), r * jnp.exp(cum), b * e_in, k2 * e_in, b * e_out, k2 * e_out, v, bonus, g)
    for o_ref, val in zip(outs, vals):
        o_ref[...] = val.astype(o_ref.dtype)
    outs[9][...] = jnp.concatenate([jnp.broadcast_to(jnp.exp(x), (SUBLANES, w)) for x in last], axis=0)
    if not has_vres:
        outs[10][...] = v


def _rwkv_prep(proj, proj_lora, prev, lp, vfirst, *, row_start, rows, seq, ret_cols, width):
    seq_mode = prev is None
    has_vres = vfirst is not None
    tp = _pick(seq, (256, 128, 64)) if seq_mode else _pick(rows, (128, 64, 32, 16, 8))
    assert row_start % tp == 0 and ret_cols % width == 0
    roff = row_start // tp
    cb = ret_cols // width
    cur = ((proj, width, cb), (proj, width, cb + 1), (proj, width, cb + 2), (proj_lora, LORA_PAD, 0))
    in_specs, args = [], []
    for arr, wd, blk in cur:
        in_specs.append(pl.BlockSpec((tp, wd), lambda i, blk=blk: (i + roff, blk)))
        args.append(arr)
    if seq_mode:
        for arr, wd, blk in cur:
            in_specs.append(pl.BlockSpec(
                (SUBLANES, wd), lambda i, blk=blk: (jnp.maximum((i + roff) * (tp // SUBLANES) - 1, 0), blk)))
            args.append(arr)
    else:
        for arr, wd, blk in ((prev[0], width, 0), (prev[0], width, 1), (prev[0], width, 2), (prev[1], LORA_PAD, 0)):
            in_specs.append(pl.BlockSpec((tp, wd), lambda i, blk=blk: (i, blk)))
            args.append(arr)

    def full(x):
        return pl.BlockSpec(x.shape, lambda i: (0, 0))

    names = ["mu", "w0", "w_up", "a0", "a_up", "g_up", "k_k", "k_a", "r_k"] + (["v0", "v_up"] if has_vres else [])
    for nm in names:
        in_specs.append(full(lp[nm]))
        args.append(lp[nm])
    if has_vres:
        in_specs.append(pl.BlockSpec((tp, width), lambda i: (i, 0)))
        args.append(vfirst)
    tok = pl.BlockSpec((tp, width), lambda i: (i, 0))
    if seq_mode:
        assert tp % RWKV_CHUNK == 0
        per_chunk = pl.BlockSpec((tp // RWKV_CHUNK * SUBLANES, width), lambda i: (i, 0))
        out_specs = [tok] * 9 + [per_chunk] + ([] if has_vres else [tok])
        out_shape = ([jax.ShapeDtypeStruct((rows, width), BF16)] * 9
                     + [jax.ShapeDtypeStruct((rows // RWKV_CHUNK * SUBLANES, width), F32)]
                     + ([] if has_vres else [jax.ShapeDtypeStruct((rows, width), F32)]))
    else:
        out_specs = [tok] * 7
        out_shape = [jax.ShapeDtypeStruct((rows, width), F32)] * 7
    return pl.pallas_call(
        functools.partial(_rwkv_prep_kernel, seq_mode=seq_mode, has_vres=has_vres,
                          tiles_per_seq=max(seq // tp, 1), width=width),
        grid=(rows // tp,),
        in_specs=in_specs,
        out_specs=out_specs,
        out_shape=out_shape,
        compiler_params=_params("parallel"),
        name="rwkv_prep",
    )(*args)


def _rwkv_scan_kernel(khat_ref, rhat_ref, btil_ref, ktil_ref, bbar_ref, kbar_ref, v_ref, bonus_ref, g_ref,
                      dtot_ref, lnw_ref, lnb_ref, y_ref, s_ref,
                      r_scr, o0_scr, m_scr, n_scr, o_scr, *, n_chunks, chunk_group, post_rows):
    c, n = RWKV_CHUNK, RWKV_HEAD_DIM
    pairs = khat_ref.shape[1] // LANES
    row = lax.broadcasted_iota(jnp.int32, (c, c), 0)
    col = lax.broadcasted_iota(jnp.int32, (c, c), 1)
    strict, incl = row > col, row >= col
    eye = jnp.where(row == col, 1.0, 0.0)
    first = lax.broadcasted_iota(jnp.int32, (c, LANES), 1) < n
    first_b = jnp.where(first, 1.0, 0.0).astype(BF16)
    second_b = jnp.where(first, 0.0, 1.0).astype(BF16)
    vrow = lax.broadcasted_iota(jnp.int32, (LANES, LANES), 0) < n
    kcol = lax.broadcasted_iota(jnp.int32, (LANES, LANES), 1) < n
    same_head = vrow == kcol

    def lanes(p):
        return slice(p * LANES, (p + 1) * LANES)

    def pass1(gi, carry):
        inst = [(gi * chunk_group + cj, p) for cj in range(chunk_group) for p in range(pairs)]
        rows = [pl.ds(pl.multiple_of(ci * c, c), c) for ci, _ in inst]
        at = [(rows[i], lanes(p)) for i, (_, p) in enumerate(inst)]
        khat = [khat_ref[a] for a in at]
        rhat = [rhat_ref[a] for a in at]
        v2 = [v_ref[a] for a in at]
        x = [jnp.concatenate([kh * first_b, kh * second_b, rh * first_b, rh * second_b], axis=0)
             for kh, rh in zip(khat, rhat)]
        ab = [_dot_nt(xi, btil_ref[a]) for xi, a in zip(x, at)]
        ak = [_dot_nt(xi, ktil_ref[a]) for xi, a in zip(x, at)]
        heads2 = [(i, hh) for i in range(len(inst)) for hh in range(2)]
        pw = [-jnp.where(strict, ab[i][hh * c:(hh + 1) * c], 0.0) for i, hh in heads2]
        a_k = [_bf(jnp.where(strict, ak[i][hh * c:(hh + 1) * c], 0.0)) for i, hh in heads2]
        a_rk = [_bf(jnp.where(incl, ak[i][(2 + hh) * c:(3 + hh) * c], 0.0)) for i, hh in heads2]
        a_rb = [_bf(jnp.where(incl, ab[i][(2 + hh) * c:(3 + hh) * c], 0.0)) for i, hh in heads2]
        akv = [_dot(a_k[j], v2[i]) for j, (i, _) in enumerate(heads2)]
        arkv = [_dot(a_rk[j], v2[i]) for j, (i, _) in enumerate(heads2)]
        akv_b = [_bf(jnp.where(first, akv[2 * i], akv[2 * i + 1])) for i in range(len(inst))]
        tinv = [eye + q for q in pw]
        for _ in range(c.bit_length() - 2):
            pwb = [_bf(q) for q in pw]
            pw = [_dot(q, q) for q in pwb]
            tinv = [t + _dot(_bf(t), _bf(q)) for t, q in zip(tinv, pw)]
        tinv = [_bf(t) for t in tinv]
        wt = [_dot(tinv[j], khat[i]) for j, (i, _) in enumerate(heads2)]
        u0 = [_dot(tinv[j], akv_b[i]) for j, (i, _) in enumerate(heads2)]
        wt_b = [_bf(jnp.where(first, wt[2 * i], wt[2 * i + 1])) for i in range(len(inst))]
        u0_b = [_bf(jnp.where(first, u0[2 * i], u0[2 * i + 1])) for i in range(len(inst))]
        bbar = [bbar_ref[a] for a in at]
        m = [_dot_tn(wt_b[i], bbar[i]) for i in range(len(inst))]
        nn = [_dot_tn(jnp.concatenate([v2[i], -u0_b[i]], axis=0),
                      jnp.concatenate([kbar_ref[at[i]], bbar[i]], axis=0)) for i in range(len(inst))]
        rw = [_dot(a_rb[j], wt_b[i]) for j, (i, _) in enumerate(heads2)]
        ru = [_dot(a_rb[j], u0_b[i]) for j, (i, _) in enumerate(heads2)]
        for i, (ci, p) in enumerate(inst):
            m_scr[ci, p] = _bf(jnp.where(same_head, -m[i], 0.0))
            n_scr[ci, p] = jnp.where(same_head, nn[i], 0.0)
            r_scr[at[i]] = _bf(rhat[i].astype(F32) - jnp.where(first, rw[2 * i], rw[2 * i + 1]))
            o0_scr[at[i]] = jnp.where(first, arkv[2 * i] - ru[2 * i], arkv[2 * i + 1] - ru[2 * i + 1])
        return carry

    lax.fori_loop(0, n_chunks // chunk_group, pass1, 0)

    def pass2(ci, states):
        rows = pl.ds(pl.multiple_of(ci * c, c), c)
        drows = pl.ds(pl.multiple_of(ci * SUBLANES, SUBLANES), SUBLANES)
        sb = [_bf(s) for s in states]
        ds = [_dot(sb[p], m_scr[ci, p]) for p in range(pairs)]
        for p in range(pairs):
            at = (rows, lanes(p))
            o_scr[at] = _dot_nt(r_scr[at], sb[p]) + o0_scr[at]
        return tuple(states[p] * dtot_ref[drows, lanes(p)][0:1] + ds[p] + n_scr[ci, p] for p in range(pairs))

    final = lax.fori_loop(0, n_chunks, pass2, tuple(jnp.zeros((LANES, LANES), F32) for _ in range(pairs)))
    for p in range(pairs):
        s_ref[0, 2 * p] = final[p][:n, :n]
        s_ref[0, 2 * p + 1] = final[p][n:, n:]

    gmat = _group_matrix()

    def post(ti, carry):
        rows = pl.ds(pl.multiple_of(ti * post_rows, post_rows), post_rows)
        o = o_scr[rows, :]
        oc = o - _group_sum(o, gmat) * (1.0 / n)
        var = _group_sum(oc * oc, gmat) * (1.0 / n)
        y = oc * lax.rsqrt(var + RWKV_LN_EPS) * lnw_ref[...] + lnb_ref[...]
        y_ref[rows, :] = ((y + bonus_ref[rows, :].astype(F32)) * g_ref[rows, :].astype(F32)).astype(y_ref.dtype)
        return carry

    lax.fori_loop(0, (n_chunks * c) // post_rows, post, 0)


def _rwkv_scan(feats, lnw, lnb, *, batch, seq, width, total_rows):
    heads = width // RWKV_HEAD_DIM
    blk = _pick(width, (2 * LANES, LANES))
    hb = blk // RWKV_HEAD_DIM
    c = RWKV_CHUNK
    assert seq % c == 0
    n_chunks = seq // c
    tok = pl.BlockSpec((seq, blk), lambda b, h: (b, h))
    par = pl.BlockSpec((1, blk), lambda b, h: (0, h))
    return pl.pallas_call(
        functools.partial(_rwkv_scan_kernel, n_chunks=n_chunks, chunk_group=_pick(n_chunks, (4, 2, 1)),
                          post_rows=_pick(seq, (256, 128, 64))),
        grid=(batch, width // blk),
        in_specs=[tok] * 9 + [pl.BlockSpec((n_chunks * SUBLANES, blk), lambda b, h: (b, h)), par, par],
        out_specs=[tok, pl.BlockSpec((1, hb, RWKV_HEAD_DIM, RWKV_HEAD_DIM), lambda b, h: (b, h, 0, 0))],
        out_shape=[jax.ShapeDtypeStruct((total_rows, width), BF16),
                   jax.ShapeDtypeStruct((batch, heads, RWKV_HEAD_DIM, RWKV_HEAD_DIM), F32)],
        scratch_shapes=[pltpu.VMEM((seq, blk), BF16), pltpu.VMEM((seq, blk), F32),
                        pltpu.VMEM((n_chunks, blk // LANES, LANES, LANES), BF16),
                        pltpu.VMEM((n_chunks, blk // LANES, LANES, LANES), F32),
                        pltpu.VMEM((seq, blk), F32)],
        compiler_params=_params("parallel", "parallel"),
        name="rwkv_scan",
    )(*feats[:10], lnw, lnb)


def _rwkv_sample_kernel(r_ref, lw_ref, k_ref, v_ref, kk_ref, b_ref, g_ref, s_ref, rk_ref, lnw_ref, lnb_ref, *rest):
    y_ref, o_ref, v_scr, out_scr = rest[-4:]
    n = RWKV_HEAD_DIM
    tokens = r_ref.shape[0]

    def chan(ref):
        return ref[...].T

    def col(ref):
        return jnp.broadcast_to(ref[...], (tokens, LANES)).T

    r, k2, v, kk, b, g = chan(r_ref), chan(k_ref), chan(v_ref), chan(kk_ref), chan(b_ref), chan(g_ref)
    w = jnp.exp(chan(lw_ref))
    v_scr[:, 0, :] = v
    for hh in range(s_ref.shape[0]):
        ch = slice(hh * n, (hh + 1) * n)
        kk_h, w_h, b_h, k_h, r_h = kk[ch], w[ch], b[ch], k2[ch], r[ch]

        def body(vi, carry):
            s = s_ref[hh, vi]
            sa = -jnp.sum(s * kk_h, axis=0, keepdims=True)
            new = s * w_h + sa * b_h + v_scr[hh * n + vi] * k_h
            o_ref[hh, vi] = new
            out_scr[hh * n + vi] = jnp.sum(new * r_h, axis=0, keepdims=True)
            return carry

        lax.fori_loop(0, n, body, 0, unroll=4)
    o = out_scr[:, 0, :]
    rkk = r * k2 * col(rk_ref)
    yn, bonus = [], []
    for hh in range(s_ref.shape[0]):
        ch = slice(hh * n, (hh + 1) * n)
        oc = o[ch] - jnp.mean(o[ch], axis=0, keepdims=True)
        var = jnp.mean(oc * oc, axis=0, keepdims=True)
        yn.append(oc * lax.rsqrt(var + RWKV_LN_EPS))
        bonus.append(jnp.sum(rkk[ch], axis=0, keepdims=True) * v[ch])
    y = (jnp.concatenate(yn, axis=0) * col(lnw_ref) + col(lnb_ref) + jnp.concatenate(bonus, axis=0)) * g
    y_ref[...] = y.T.astype(y_ref.dtype)


def _rwkv_sample(feats, states_t, layer, y_buf, new_states, rk, lnw, lnb, *, row_start, width):
    heads, bs = states_t.shape[1], states_t.shape[4]
    hb = LANES // RWKV_HEAD_DIM
    assert row_start % bs == 0 and heads % hb == 0
    roff = row_start // bs
    tok = pl.BlockSpec((bs, LANES), lambda j: (0, j))
    par = pl.BlockSpec((1, LANES), lambda j: (0, j))
    st = pl.BlockSpec((None, hb, RWKV_HEAD_DIM, RWKV_HEAD_DIM, bs), lambda j: (layer, j, 0, 0, 0))
    in_specs = [tok] * 7 + [st] + [par] * 3
    alias_specs, alias_args, aliases = _in_place([y_buf, new_states], len(in_specs))
    return pl.pallas_call(
        _rwkv_sample_kernel,
        grid=(heads // hb,),
        in_specs=in_specs + alias_specs,
        out_specs=[pl.BlockSpec((bs, LANES), lambda j: (roff, j)), st],
        out_shape=[jax.ShapeDtypeStruct(y_buf.shape, y_buf.dtype),
                   jax.ShapeDtypeStruct(states_t.shape, states_t.dtype)],
        scratch_shapes=[pltpu.VMEM((LANES, 1, bs), F32), pltpu.VMEM((LANES, 1, bs), F32)],
        input_output_aliases=aliases,
        compiler_params=_params("parallel"),
        name="rwkv_sample",
    )(*feats, states_t, rk, lnw, lnb, *alias_args)


def _rope_tables(pos):
    half = RET_HEAD_DIM // 2
    inv_freq = ROPE_BASE ** (-jnp.arange(half, dtype=F32) / half)
    ang = pos[:, None] * inv_freq[None, :]
    cos, sin = jnp.cos(ang), jnp.sin(ang)
    return jnp.concatenate([cos, cos], axis=-1), jnp.concatenate([-sin, sin], axis=-1)


def _pad_rows(w, start, total):
    return jnp.zeros((total, w.shape[1]), BF16).at[start:start + w.shape[0]].set(_bf(w))


def kernel(x_prompt, x_sample, state_ret, state_rwkv, state_shift, w_in, w_in_vres, mu_shift, mu_shift_vres, ret_ln_w, rwkv_w0, rwkv_w_up, rwkv_a0, rwkv_a_up, rwkv_g_up, rwkv_v0, rwkv_v_up, rwkv_k_k, rwkv_k_a, rwkv_r_k, rwkv_ln_w, rwkv_ln_b, w_out, g_attn, g_ffn, w_gate, w_up, w_down, g_final):
    batch, seq, d = x_prompt.shape
    bs = x_sample.shape[0]
    assert x_sample.shape[1] == 1
    depth = w_in.shape[0]
    mp = batch * seq
    ret_w = d // 2
    rw_w = d - ret_w
    ret_cols = 4 * ret_w
    main_cols = ret_cols + 3 * rw_w
    r_decay, r_aaa, r_gate = rwkv_w_up.shape[1], rwkv_a_up.shape[1], rwkv_g_up.shape[1]
    r_mv = rwkv_v_up.shape[1]
    o_a, o_g = r_decay, r_decay + r_aaa
    o_v = o_g + r_gate
    assert o_v + r_mv <= LORA_PAD and w_in.shape[2] == main_cols + o_v
    ret_heads = ret_w // RET_HEAD_DIM

    log_gamma = jnp.log1p(-jnp.exp2(-5.0 - jnp.arange(ret_heads, dtype=F32)))
    lg = jnp.broadcast_to(log_gamma[:, None, None], (ret_heads, 1, LANES))
    cos_p, sin_p = _rope_tables(jnp.arange(seq, dtype=F32))
    cos_s, sin_s = _rope_tables(PAST_LEN + jnp.arange(1, dtype=F32))

    pad = LORA_PAD - o_v - r_mv
    w_in_t = jnp.transpose(w_in, (0, 2, 1))
    w_lora = jnp.concatenate(
        [w_in[:, :, main_cols:], jnp.concatenate([jnp.zeros((1, d, r_mv), F32), w_in_vres], axis=0),
         jnp.zeros((depth, d, pad), F32)], axis=-1)
    mu_all = jnp.concatenate(
        [mu_shift, jnp.concatenate([jnp.zeros((1, r_mv), F32), mu_shift_vres], axis=0), jnp.zeros((depth, pad), F32)],
        axis=-1)
    state_rwkv_t = jnp.transpose(state_rwkv, (0, 2, 3, 4, 1))

    h = jnp.concatenate([x_prompt.reshape(mp, d), x_sample.reshape(bs, d)], axis=0)
    vfirst_p = vfirst_s = None
    ret_p, rwkv_p, shift_p, shift_s = [], [], [], []
    ret_s = rwkv_s = None
    for l in range(depth):
        lp = {
            "mu": mu_all[l].reshape(1, -1),
            "w0": rwkv_w0[l].reshape(1, -1), "w_up": _pad_rows(rwkv_w_up[l], 0, LORA_PAD),
            "a0": rwkv_a0[l].reshape(1, -1), "a_up": _pad_rows(rwkv_a_up[l], o_a, LORA_PAD),
            "g_up": _pad_rows(rwkv_g_up[l], o_g, LORA_PAD),
            "k_k": rwkv_k_k[l].reshape(1, -1), "k_a": rwkv_k_a[l].reshape(1, -1),
            "r_k": rwkv_r_k[l].reshape(1, -1),
        }
        if l > 0:
            lp["v0"] = rwkv_v0[l - 1].reshape(1, -1)
            lp["v_up"] = _pad_rows(rwkv_v_up[l - 1], o_v, LORA_PAD)
        lnw, lnb = rwkv_ln_w[l].reshape(1, -1), rwkv_ln_b[l].reshape(1, -1)
        ret_lnw = ret_ln_w[l].reshape(1, -1)

        xn = _rmsnorm(h, g_attn[l], BF16)
        proj = _matmul([(xn, w_in_t, 0)], l, main_cols, F32, transposed=True, name="in_proj")
        proj_lora = _matmul([(xn, w_lora, 0)], l, LORA_PAD, F32, name="in_proj_lora")
        h_last = jnp.concatenate([h[seq - 1:mp:seq], h[mp:]], axis=0)
        xn_last = _rmsnorm(h_last, g_attn[l], F32)
        shift_p.append(xn_last[:batch])
        shift_s.append(xn_last[batch:])
        prev_tok = _bf(state_shift[l])
        prev_s = (_matmul([(prev_tok, w_in_t, 0)], l, 3 * rw_w, F32, col_start=ret_cols, transposed=True,
                          name="prev_proj"),
                  _matmul([(prev_tok, w_lora, 0)], l, LORA_PAD, F32, name="prev_proj_lora"))

        feats_p = _rwkv_prep(proj, proj_lora, None, lp, vfirst_p, row_start=0, rows=mp, seq=seq,
                             ret_cols=ret_cols, width=rw_w)
        feats_s = _rwkv_prep(proj, proj_lora, prev_s, lp, vfirst_s, row_start=mp, rows=bs, seq=1,
                             ret_cols=ret_cols, width=rw_w)
        if l == 0:
            vfirst_p, vfirst_s = feats_p[10], feats_s[3]

        y_ret, s_ret_p = _ret_prompt(proj, cos_p, sin_p, lg, ret_lnw, batch=batch, seq=seq, width=ret_w,
                                     total_rows=mp + bs)
        y_ret, ret_s = _ret_sample(proj, state_ret, l, y_ret, ret_s, cos_s, sin_s, lg, ret_lnw,
                                   row_start=mp, width=ret_w)
        y_rw, s_rw_p = _rwkv_scan(feats_p, lnw, lnb, batch=batch, seq=seq, width=rw_w, total_rows=mp + bs)
        y_rw, rwkv_s = _rwkv_sample(feats_s, state_rwkv_t, l, y_rw, rwkv_s, lp["r_k"], lnw, lnb,
                                    row_start=mp, width=rw_w)
        ret_p.append(s_ret_p)
        rwkv_p.append(s_rw_p)

        h = _matmul([(y_ret, w_out, 0), (y_rw, w_out, ret_w)], l, d, F32, res=h, name="out_proj")
        hn = _rmsnorm(h, g_ffn[l], BF16)
        act = _matmul([(hn, w_up, 0)], l, w_up.shape[2], BF16, gate=[w_gate], name="swiglu")
        h = _matmul([(act, w_down, 0)], l, d, F32, res=h, tm=_pick(mp + bs, FFN_DOWN_ROW_TILES), name="ffn_down")

    y_p = _rmsnorm(h, g_final, F32, row_start=0, rows=mp).reshape(batch, seq, d)
    y_s = _rmsnorm(h, g_final, F32, row_start=mp, rows=bs, tm=_pick(bs, (128, 64, 32, 16, 8))).reshape(bs, 1, d)
    return (y_p, y_s, jnp.stack(ret_p), jnp.stack(rwkv_p), jnp.stack(shift_p),
            ret_s, jnp.transpose(rwkv_s, (0, 4, 1, 2, 3)), jnp.stack(shift_s))
```

```python
import functools

import jax
import jax.numpy as jnp
from jax import lax
from jax.experimental import pallas as pl
from jax.experimental.pallas import tpu as pltpu

F32 = jnp.float32
BF16 = jnp.bfloat16

PAST_LEN = 16384
ROPE_BASE = 10000.0
RET_HEAD_DIM = 128
RET_CHUNK = 128
RWKV_HEAD_DIM = 64
RWKV_CHUNK = 64
NORM_EPS = 1e-6
RET_LN_EPS = 1e-5
RWKV_LN_EPS = 64e-5

LANES = 128
SUBLANES = 8
LORA_PAD = 512
VMEM_LIMIT_BYTES = 56 * 1024 * 1024
ROW_TILES = (1664, 832, 640, 512, 256, 128, 64, 32, 16)
FFN_DOWN_ROW_TILES = ROW_TILES[2:]
STEP_TOKENS = 8


def _params(*sem):
    return pltpu.CompilerParams(dimension_semantics=sem, vmem_limit_bytes=VMEM_LIMIT_BYTES)


def _pick(n, candidates):
    for c in candidates:
        if n % c == 0:
            return c
    raise ValueError(f"no tile in {candidates} divides {n}")


def _dot(a, b):
    return jnp.dot(a, b, preferred_element_type=F32)


def _dot_nt(a, b):
    return lax.dot_general(a, b, (((1,), (1,)), ((), ())), preferred_element_type=F32)


def _dot_tn(a, b):
    return lax.dot_general(a, b, (((0,), (0,)), ((), ())), preferred_element_type=F32)


def _bf(x):
    return x.astype(BF16)


def _split2(x):
    hi = _bf(x)
    return hi, _bf(x - hi.astype(F32))


def _rmsnorm_kernel(x_ref, g_ref, o_ref):
    x = x_ref[...]
    y = x * lax.rsqrt(jnp.mean(x * x, axis=-1, keepdims=True) + NORM_EPS)
    o_ref[...] = (y * g_ref[...]).astype(o_ref.dtype)


def _rmsnorm(x, g, out_dtype, *, row_start=0, rows=None, tm=None):
    m, d = x.shape
    rows = m if rows is None else rows
    if tm is None:
        tm = rows if rows <= 640 else _pick(rows, (640, 512, 256, 128, 64, 32, 16))
    assert row_start % tm == 0 and rows % tm == 0
    off = row_start // tm
    return pl.pallas_call(
        _rmsnorm_kernel,
        grid=(rows // tm,),
        in_specs=[pl.BlockSpec((tm, d), lambda i: (i + off, 0)), pl.BlockSpec((1, d), lambda i: (0, 0))],
        out_specs=pl.BlockSpec((tm, d), lambda i: (i, 0)),
        out_shape=jax.ShapeDtypeStruct((rows, d), out_dtype),
        compiler_params=_params("parallel"),
        name="rmsnorm",
    )(x, g.reshape(1, d))


def _mm_kernel(*refs, n_pairs, has_res, transposed, swiglu, row_splits):
    n_w = 2 * n_pairs if swiglu else n_pairs
    a_refs, w_refs = refs[:n_pairs], refs[n_pairs:n_pairs + n_w]
    w_scr = refs[len(refs) - n_w:]
    o_ref = refs[len(refs) - n_w - 1]

    @pl.when(pl.program_id(1) == 0)
    def _():
        for w_ref, scr in zip(w_refs, w_scr):
            scr[...] = w_ref[...].astype(BF16)

    def product(scrs, rows):
        acc = None
        for a_ref, scr in zip(a_refs, scrs):
            d = _dot_nt(a_ref[rows, :], scr[...]) if transposed else _dot(a_ref[rows, :], scr[...])
            acc = d if acc is None else acc + d
        return acc

    tm = o_ref.shape[0]
    sub = tm // row_splits
    for r in range(row_splits):
        rows = slice(r * sub, (r + 1) * sub)
        acc = product(w_scr[:n_pairs], rows)
        if swiglu:
            acc = acc * jax.nn.sigmoid(acc) * product(w_scr[n_pairs:], rows)
        if has_res:
            acc = acc + refs[n_pairs + n_w][rows, :]
        o_ref[rows, :] = acc.astype(o_ref.dtype)


def _matmul(pairs, layer, n, out_dtype, *, res=None, col_start=0, transposed=False, gate=None, tm=None, tn=512,
            name="matmul"):
    m = pairs[0][0].shape[0]
    tm = _pick(m, ROW_TILES) if tm is None else tm
    assert col_start % tn == 0 and n % tn == 0 and m % tm == 0
    coff = col_start // tn
    in_specs, args, scratch = [], [], []
    for a, _, _ in pairs:
        in_specs.append(pl.BlockSpec((tm, a.shape[1]), lambda j, i: (i, 0)))
        args.append(a)
    weights = [(a.shape[1], g, k_start) for (a, _, k_start), g in zip(pairs, gate or [])]
    weights += [(a.shape[1], w, k_start) for a, w, k_start in pairs]
    for k, w, k_start in weights:
        assert k_start % k == 0
        kb = k_start // k
        if transposed:
            in_specs.append(pl.BlockSpec((None, tn, k), lambda j, i, kb=kb: (layer, j + coff, kb)))
            scratch.append(pltpu.VMEM((tn, k), BF16))
        else:
            in_specs.append(pl.BlockSpec((None, k, tn), lambda j, i, kb=kb: (layer, kb, j + coff)))
            scratch.append(pltpu.VMEM((k, tn), BF16))
        args.append(w)
    if res is not None:
        in_specs.append(pl.BlockSpec((tm, tn), lambda j, i: (i, j)))
        args.append(res)
    return pl.pallas_call(
        functools.partial(_mm_kernel, n_pairs=len(pairs), has_res=res is not None, transposed=transposed,
                          swiglu=gate is not None, row_splits=_pick(tm // 16, (4, 2, 1))),
        grid=(n // tn, m // tm),
        in_specs=in_specs,
        out_specs=pl.BlockSpec((tm, tn), lambda j, i: (i, j)),
        out_shape=jax.ShapeDtypeStruct((m, n), out_dtype),
        scratch_shapes=scratch,
        compiler_params=_params("parallel", "arbitrary"),
        name=name,
    )(*args)


def _rope(x, cos, sin_signed):
    return x * cos + pltpu.roll(x, RET_HEAD_DIM // 2, 1) * sin_signed


def _head_norm(y, eps):
    mu = jnp.mean(y, axis=-1, keepdims=True)
    yc = y - mu
    var = jnp.mean(yc * yc, axis=-1, keepdims=True)
    return yc * lax.rsqrt(var + eps)


def _ret_prompt_kernel(q_ref, k_ref, v_ref, g_ref, cos_ref, sin_ref, lg_ref, lnw_ref, y_ref, s_ref, *,
                       n_chunks, group):
    c, d = RET_CHUNK, RET_HEAD_DIM
    lg = lg_ref[0]
    row = lax.broadcasted_iota(jnp.int32, (c, c), 0)
    col = lax.broadcasted_iota(jnp.int32, (c, c), 1)
    diff = (row - col).astype(F32)
    decay_in = jnp.where(diff >= 0, jnp.exp(lg * jnp.maximum(diff, 0.0)), 0.0)
    ridx = lax.broadcasted_iota(jnp.int32, (c, d), 0).astype(F32)
    tail = jnp.exp((c - 1 - ridx) * lg)
    head = jnp.exp((ridx + 1) * lg)
    chunk_decay = jnp.exp(c * lg)
    lnw = lnw_ref[...]

    def body(gi, s):
        rows = [pl.ds(pl.multiple_of((gi * group + j) * c, c), c) for j in range(group)]
        cs = [(cos_ref[r, :], sin_ref[r, :]) for r in rows]
        q = [_rope(q_ref[r, :], co, si) for r, (co, si) in zip(rows, cs)]
        k = [_rope(k_ref[r, :], co, si) * (d ** -0.5) for r, (co, si) in zip(rows, cs)]
        vb = [_bf(v_ref[r, :]) for r in rows]
        scores = [_dot_nt(_bf(qj), _bf(kj)) * decay_in for qj, kj in zip(q, k)]
        kv = [_dot_tn(_bf(kj * tail), vj) for kj, vj in zip(k, vb)]
        states = [s]
        for j in range(group):
            states.append(states[j] * chunk_decay + kv[j])
        y = [_dot(_bf(sc), vj) + _dot(_bf(qj * head), _bf(sj))
             for sc, vj, qj, sj in zip(scores, vb, q, states)]
        for r, yj in zip(rows, y):
            g = g_ref[r, :]
            y_ref[r, :] = (g * jax.nn.sigmoid(g) * (_head_norm(yj, RET_LN_EPS) * lnw)).astype(y_ref.dtype)
        return states[group]

    s_ref[0, 0] = lax.fori_loop(0, n_chunks // group, body, jnp.zeros((d, d), F32))


def _ret_prompt(proj, cos, sin, lg, ln_w, *, batch, seq, width, total_rows):
    heads = width // RET_HEAD_DIM
    assert seq % RET_CHUNK == 0 and RET_CHUNK == LANES
    n_chunks = seq // RET_CHUNK

    def col(which):
        return pl.BlockSpec((seq, RET_HEAD_DIM), lambda b, h: (b, which * heads + h))

    return pl.pallas_call(
        functools.partial(_ret_prompt_kernel, n_chunks=n_chunks, group=_pick(n_chunks, (8, 4, 2, 1))),
        grid=(batch, heads),
        in_specs=[col(0), col(1), col(2), col(3),
                  pl.BlockSpec((seq, RET_HEAD_DIM), lambda b, h: (0, 0)),
                  pl.BlockSpec((seq, RET_HEAD_DIM), lambda b, h: (0, 0)),
                  pl.BlockSpec((1, 1, LANES), lambda b, h: (h, 0, 0)),
                  pl.BlockSpec((1, RET_HEAD_DIM), lambda b, h: (0, h))],
        out_specs=[pl.BlockSpec((seq, RET_HEAD_DIM), lambda b, h: (b, h)),
                   pl.BlockSpec((1, 1, RET_HEAD_DIM, RET_HEAD_DIM), lambda b, h: (b, h, 0, 0))],
        out_shape=[jax.ShapeDtypeStruct((total_rows, width), BF16),
                   jax.ShapeDtypeStruct((batch, heads, RET_HEAD_DIM, RET_HEAD_DIM), F32)],
        compiler_params=_params("parallel", "parallel"),
        name="ret_prompt",
    )(proj, proj, proj, proj, cos, sin, lg, ln_w)


def _ret_sample_kernel(q_ref, k_ref, v_ref, g_ref, s_ref, cos_ref, sin_ref, lg_ref, lnw_ref, *rest):
    y_ref, o_ref, q_scr, kp_scr, v_scr, y_scr = rest[-6:]
    d = RET_HEAD_DIM
    tt, width = q_ref.shape
    heads = width // d
    cos, sin = cos_ref[...], sin_ref[...]
    v_scr[:, 0, :] = v_ref[...]
    for h in range(heads):
        ln = slice(h * d, (h + 1) * d)
        q_scr[:, 0, ln] = _rope(q_ref[:, ln], cos, sin)
        k = _rope(k_ref[:, ln], cos, sin) * (d ** -0.5)
        k_hi = _bf(k).astype(F32)
        kp_scr[:, 0, 2 * h * d:(2 * h + 1) * d] = k_hi
        kp_scr[:, 0, (2 * h + 1) * d:(2 * h + 2) * d] = k - k_hi
    row = lax.broadcasted_iota(jnp.int32, (d, 2 * d), 0)
    col = lax.broadcasted_iota(jnp.int32, (d, 2 * d), 1)
    eye2 = jnp.where((col == row) | (col == row + d), 1.0, 0.0).astype(BF16)

    def body(ti, carry):
        th = [(2 * ti + j, h) for j in range(2) for h in range(heads)]
        lns = [slice(h * d, (h + 1) * d) for _, h in th]
        kp = [_bf(jnp.broadcast_to(kp_scr[t, :, 2 * h * d:(2 * h + 2) * d], (d, 2 * d))) for t, h in th]
        k_col = [_dot_nt(eye2, x) for x in kp]
        s_new = [s_ref[t, h] * jnp.exp(lg_ref[h]) + kc * v_scr[t, :, ln] for (t, h), kc, ln in zip(th, k_col, lns)]
        for (t, h), s in zip(th, s_new):
            o_ref[t, h] = s
        y = [_dot(_bf(jnp.broadcast_to(q_scr[t, :, ln], (SUBLANES, d))), _bf(s))
             for (t, h), s, ln in zip(th, s_new, lns)]
        for (t, h), yi, ln in zip(th, y, lns):
            y_scr[t, :, ln] = yi[0:1]
        return carry

    assert tt % 2 == 0
    lax.fori_loop(0, tt // 2, body, 0)
    for h in range(heads):
        ln = slice(h * d, (h + 1) * d)
        g = g_ref[:, ln]
        yn = _head_norm(y_scr[:, 0, ln], RET_LN_EPS) * lnw_ref[:, ln]
        y_ref[:, ln] = (g * jax.nn.sigmoid(g) * yn).astype(y_ref.dtype)


def _in_place(bufs, n_inputs):
    specs, args, aliases = [], [], {}
    for out_idx, buf in enumerate(bufs):
        if buf is not None:
            aliases[n_inputs + len(args)] = out_idx
            specs.append(pl.BlockSpec(memory_space=pl.ANY))
            args.append(buf)
    return specs, args, aliases


def _ret_sample(proj, states, layer, y_buf, new_states, cos, sin, lg, ln_w, *, row_start, width):
    bs, heads = states.shape[1], states.shape[2]
    tt = STEP_TOKENS
    assert row_start % tt == 0 and bs % tt == 0
    roff = row_start // tt

    def col(which):
        return pl.BlockSpec((tt, width), lambda i: (i + roff, which))

    st = pl.BlockSpec((None, tt, heads, RET_HEAD_DIM, RET_HEAD_DIM), lambda i: (layer, i, 0, 0, 0))
    in_specs = [col(0), col(1), col(2), col(3), st,
                pl.BlockSpec((1, RET_HEAD_DIM), lambda i: (0, 0)),
                pl.BlockSpec((1, RET_HEAD_DIM), lambda i: (0, 0)),
                pl.BlockSpec((heads, 1, LANES), lambda i: (0, 0, 0)),
                pl.BlockSpec((1, width), lambda i: (0, 0))]
    alias_specs, alias_args, aliases = _in_place([y_buf, new_states], len(in_specs))
    return pl.pallas_call(
        _ret_sample_kernel,
        grid=(bs // tt,),
        in_specs=in_specs + alias_specs,
        out_specs=[pl.BlockSpec((tt, width), lambda i: (i + roff, 0)), st],
        out_shape=[jax.ShapeDtypeStruct(y_buf.shape, y_buf.dtype), jax.ShapeDtypeStruct(states.shape, states.dtype)],
        scratch_shapes=[pltpu.VMEM((tt, 1, width), F32), pltpu.VMEM((tt, 1, 2 * width), F32),
                        pltpu.VMEM((tt, 1, width), F32), pltpu.VMEM((tt, 1, width), F32)],
        input_output_aliases=aliases,
        compiler_params=_params("parallel"),
        name="ret_sample",
    )(proj, proj, proj, proj, states, cos, sin, lg, ln_w, *alias_args)


def _softplus(x):
    return jnp.maximum(x, 0.0) + jnp.log1p(jnp.exp(-jnp.abs(x)))


def _group_matrix():
    r = lax.broadcasted_iota(jnp.int32, (LANES, LANES), 0) // RWKV_HEAD_DIM
    c = lax.broadcasted_iota(jnp.int32, (LANES, LANES), 1) // RWKV_HEAD_DIM
    return jnp.where(r == c, 1.0, 0.0).astype(BF16)


def _group_sum(x, gmat):
    outs = []
    for j in range(x.shape[1] // LANES):
        hi, lo = _split2(x[:, j * LANES:(j + 1) * LANES])
        outs.append(_dot(hi, gmat) + _dot(lo, gmat))
    return outs[0] if len(outs) == 1 else jnp.concatenate(outs, axis=1)


def _cumsum_rows(x, tri):
    hi = _bf(x)
    r1 = x - hi.astype(F32)
    mid = _bf(r1)
    lo = _bf(r1 - mid.astype(F32))
    return _dot(tri, hi) + _dot(tri, mid) + _dot(tri, lo)


def _rwkv_prep_kernel(*refs, seq_mode, has_vres, tiles_per_seq, width):
    it = iter(refs)
    cur = [next(it) for _ in range(4)]
    prev = [next(it) for _ in range(4)]
    mu_ref, w0_ref, wup_ref, a0_ref, aup_ref, gup_ref, kk_ref, ka_ref, rk_ref = (next(it) for _ in range(9))
    if has_vres:
        v0_ref, vup_ref, vfirst_ref = (next(it) for _ in range(3))
    outs = list(it)
    i = pl.program_id(0)

    def shifted(idx, mu):
        p = cur[idx][...]
        if seq_mode:
            carry = jnp.where(i % tiles_per_seq == 0, 0.0, prev[idx][SUBLANES - 1:SUBLANES, :])
            rowi = lax.broadcasted_iota(jnp.int32, p.shape, 0)
            p_prev = jnp.where(rowi == 0, carry, pltpu.roll(p, 1, 0))
        else:
            p_prev = prev[idx][...]
        return p + (p_prev - p) * mu

    w = width
    r = shifted(0, mu_ref[:, 0:w])
    k = shifted(1, mu_ref[:, w:2 * w])
    v = shifted(2, mu_ref[:, 2 * w:3 * w])
    lo = shifted(3, mu_ref[:, 3 * w:])
    lob = _bf(lo)
    w_raw = -_softplus(-(w0_ref[...] + _dot(_bf(jnp.tanh(lo)), wup_ref[...]))) - 0.5
    lw = -jnp.exp(w_raw)
    a = jax.nn.sigmoid(a0_ref[...] + _dot(lob, aup_ref[...]))
    g = _dot(_bf(jax.nn.sigmoid(lo)), gup_ref[...])
    if has_vres:
        v = v + (vfirst_ref[...] - v) * jax.nn.sigmoid(v0_ref[...] + _dot(lob, vup_ref[...]))
    k2 = k * (1.0 + (a - 1.0) * ka_ref[...])
    kk = k * kk_ref[...]
    gmat = _group_matrix()
    kk = kk / jnp.maximum(jnp.sqrt(_group_sum(kk * kk, gmat)), 1e-12)
    b = kk * a
    if not seq_mode:
        for o_ref, val in zip(outs, (r, lw, k2, v, kk, b, g)):
            o_ref[...] = val
        return

    tp, c = r.shape[0], RWKV_CHUNK
    row = lax.broadcasted_iota(jnp.int32, (tp, tp), 0)
    col = lax.broadcasted_iota(jnp.int32, (tp, tp), 1)
    tri = jnp.where(row >= col, jnp.where(row // c == col // c, 1.0, 0.0), 0.0).astype(BF16)
    cum = _cumsum_rows(lw, tri)
    last = [cum[j * c + c - 1:j * c + c, :] for j in range(tp // c)]
    tot = jnp.concatenate([jnp.broadcast_to(x, (c, w)) for x in last], axis=0)
    e_in = jnp.exp(-cum)
    e_out = jnp.exp(tot - cum)
    bonus = _group_sum(r * k2 * rk_ref[...], gmat) * v
    vals = (kk * jnp.exp(cum - lw), r * jnp.exp(cum), b * e_in, k2 * e_in, b * e_out, k2 * e_out, v, bonus, g)
    for o_ref, val in zip(outs, vals):
        o_ref[...] = val.astype(o_ref.dtype)
    outs[9][...] = jnp.concatenate([jnp.broadcast_to(jnp.exp(x), (SUBLANES, w)) for x in last], axis=0)
    if not has_vres:
        outs[10][...] = v


def _rwkv_prep(proj, proj_lora, prev, lp, vfirst, *, row_start, rows, seq, ret_cols, width):
    seq_mode = prev is None
    has_vres = vfirst is not None
    tp = _pick(seq, (256, 128, 64)) if seq_mode else _pick(rows, (128, 64, 32, 16, 8))
    assert row_start % tp == 0 and ret_cols % width == 0
    roff = row_start // tp
    cb = ret_cols // width
    cur = ((proj, width, cb), (proj, width, cb + 1), (proj, width, cb + 2), (proj_lora, LORA_PAD, 0))
    in_specs, args = [], []
    for arr, wd, blk in cur:
        in_specs.append(pl.BlockSpec((tp, wd), lambda i, blk=blk: (i + roff, blk)))
        args.append(arr)
    if seq_mode:
        for arr, wd, blk in cur:
            in_specs.append(pl.BlockSpec(
                (SUBLANES, wd), lambda i, blk=blk: (jnp.maximum((i + roff) * (tp // SUBLANES) - 1, 0), blk)))
            args.append(arr)
    else:
        for arr, wd, blk in ((prev[0], width, 0), (prev[0], width, 1), (prev[0], width, 2), (prev[1], LORA_PAD, 0)):
            in_specs.append(pl.BlockSpec((tp, wd), lambda i, blk=blk: (i, blk)))
            args.append(arr)

    def full(x):
        return pl.BlockSpec(x.shape, lambda i: (0, 0))

    names = ["mu", "w0", "w_up", "a0", "a_up", "g_up", "k_k", "k_a", "r_k"] + (["v0", "v_up"] if has_vres else [])
    for nm in names:
        in_specs.append(full(lp[nm]))
        args.append(lp[nm])
    if has_vres:
        in_specs.append(pl.BlockSpec((tp, width), lambda i: (i, 0)))
        args.append(vfirst)
    tok = pl.BlockSpec((tp, width), lambda i: (i, 0))
    if seq_mode:
        assert tp % RWKV_CHUNK == 0
        per_chunk = pl.BlockSpec((tp // RWKV_CHUNK * SUBLANES, width), lambda i: (i, 0))
        out_specs = [tok] * 9 + [per_chunk] + ([] if has_vres else [tok])
        out_shape = ([jax.ShapeDtypeStruct((rows, width), BF16)] * 9
                     + [jax.ShapeDtypeStruct((rows // RWKV_CHUNK * SUBLANES, width), F32)]
                     + ([] if has_vres else [jax.ShapeDtypeStruct((rows, width), F32)]))
    else:
        out_specs = [tok] * 7
        out_shape = [jax.ShapeDtypeStruct((rows, width), F32)] * 7
    return pl.pallas_call(
        functools.partial(_rwkv_prep_kernel, seq_mode=seq_mode, has_vres=has_vres,
                          tiles_per_seq=max(seq // tp, 1), width=width),
        grid=(rows // tp,),
        in_specs=in_specs,
        out_specs=out_specs,
        out_shape=out_shape,
        compiler_params=_params("parallel"),
        name="rwkv_prep",
    )(*args)


def _rwkv_scan_kernel(khat_ref, rhat_ref, btil_ref, ktil_ref, bbar_ref, kbar_ref, v_ref, bonus_ref, g_ref,
                      dtot_ref, lnw_ref, lnb_ref, y_ref, s_ref,
                      r_scr, o0_scr, m_scr, n_scr, o_scr, *, n_chunks, chunk_group, post_rows):
    c, n = RWKV_CHUNK, RWKV_HEAD_DIM
    pairs = khat_ref.shape[1] // LANES
    row = lax.broadcasted_iota(jnp.int32, (c, c), 0)
    col = lax.broadcasted_iota(jnp.int32, (c, c), 1)
    strict, incl = row > col, row >= col
    eye = jnp.where(row == col, 1.0, 0.0)
    first = lax.broadcasted_iota(jnp.int32, (c, LANES), 1) < n
    first_b = jnp.where(first, 1.0, 0.0).astype(BF16)
    second_b = jnp.where(first, 0.0, 1.0).astype(BF16)
    vrow = lax.broadcasted_iota(jnp.int32, (LANES, LANES), 0) < n
    kcol = lax.broadcasted_iota(jnp.int32, (LANES, LANES), 1) < n
    same_head = vrow == kcol

    def lanes(p):
        return slice(p * LANES, (p + 1) * LANES)

    def pass1(gi, carry):
        inst = [(gi * chunk_group + cj, p) for cj in range(chunk_group) for p in range(pairs)]
        rows = [pl.ds(pl.multiple_of(ci * c, c), c) for ci, _ in inst]
        at = [(rows[i], lanes(p)) for i, (_, p) in enumerate(inst)]
        khat = [khat_ref[a] for a in at]
        rhat = [rhat_ref[a] for a in at]
        v2 = [v_ref[a] for a in at]
        x = [jnp.concatenate([kh * first_b, kh * second_b, rh * first_b, rh * second_b], axis=0)
             for kh, rh in zip(khat, rhat)]
        ab = [_dot_nt(xi, btil_ref[a]) for xi, a in zip(x, at)]
        ak = [_dot_nt(xi, ktil_ref[a]) for xi, a in zip(x, at)]
        heads2 = [(i, hh) for i in range(len(inst)) for hh in range(2)]
        pw = [-jnp.where(strict, ab[i][hh * c:(hh + 1) * c], 0.0) for i, hh in heads2]
        a_k = [_bf(jnp.where(strict, ak[i][hh * c:(hh + 1) * c], 0.0)) for i, hh in heads2]
        a_rk = [_bf(jnp.where(incl, ak[i][(2 + hh) * c:(3 + hh) * c], 0.0)) for i, hh in heads2]
        a_rb = [_bf(jnp.where(incl, ab[i][(2 + hh) * c:(3 + hh) * c], 0.0)) for i, hh in heads2]
        akv = [_dot(a_k[j], v2[i]) for j, (i, _) in enumerate(heads2)]
        arkv = [_dot(a_rk[j], v2[i]) for j, (i, _) in enumerate(heads2)]
        akv_b = [_bf(jnp.where(first, akv[2 * i], akv[2 * i + 1])) for i in range(len(inst))]
        tinv = [eye + q for q in pw]
        for _ in range(c.bit_length() - 2):
            pwb = [_bf(q) for q in pw]
            pw = [_dot(q, q) for q in pwb]
            tinv = [t + _dot(_bf(t), _bf(q)) for t, q in zip(tinv, pw)]
        tinv = [_bf(t) for t in tinv]
        wt = [_dot(tinv[j], khat[i]) for j, (i, _) in enumerate(heads2)]
        u0 = [_dot(tinv[j], akv_b[i]) for j, (i, _) in enumerate(heads2)]
        wt_b = [_bf(jnp.where(first, wt[2 * i], wt[2 * i + 1])) for i in range(len(inst))]
        u0_b = [_bf(jnp.where(first, u0[2 * i], u0[2 * i + 1])) for i in range(len(inst))]
        bbar = [bbar_ref[a] for a in at]
        m = [_dot_tn(wt_b[i], bbar[i]) for i in range(len(inst))]
        nn = [_dot_tn(jnp.concatenate([v2[i], -u0_b[i]], axis=0),
                      jnp.concatenate([kbar_ref[at[i]], bbar[i]], axis=0)) for i in range(len(inst))]
        rw = [_dot(a_rb[j], wt_b[i]) for j, (i, _) in enumerate(heads2)]
        ru = [_dot(a_rb[j], u0_b[i]) for j, (i, _) in enumerate(heads2)]
        for i, (ci, p) in enumerate(inst):
            m_scr[ci, p] = _bf(jnp.where(same_head, -m[i], 0.0))
            n_scr[ci, p] = jnp.where(same_head, nn[i], 0.0)
            r_scr[at[i]] = _bf(rhat[i].astype(F32) - jnp.where(first, rw[2 * i], rw[2 * i + 1]))
            o0_scr[at[i]] = jnp.where(first, arkv[2 * i] - ru[2 * i], arkv[2 * i + 1] - ru[2 * i + 1])
        return carry

    lax.fori_loop(0, n_chunks // chunk_group, pass1, 0)

    def pass2(ci, states):
        rows = pl.ds(pl.multiple_of(ci * c, c), c)
        drows = pl.ds(pl.multiple_of(ci * SUBLANES, SUBLANES), SUBLANES)
        sb = [_bf(s) for s in states]
        ds = [_dot(sb[p], m_scr[ci, p]) for p in range(pairs)]
        for p in range(pairs):
            at = (rows, lanes(p))
            o_scr[at] = _dot_nt(r_scr[at], sb[p]) + o0_scr[at]
        return tuple(states[p] * dtot_ref[drows, lanes(p)][0:1] + ds[p] + n_scr[ci, p] for p in range(pairs))

    final = lax.fori_loop(0, n_chunks, pass2, tuple(jnp.zeros((LANES, LANES), F32) for _ in range(pairs)))
    for p in range(pairs):
        s_ref[0, 2 * p] = final[p][:n, :n]
        s_ref[0, 2 * p + 1] = final[p][n:, n:]

    gmat = _group_matrix()

    def post(ti, carry):
        rows = pl.ds(pl.multiple_of(ti * post_rows, post_rows), post_rows)
        o = o_scr[rows, :]
        oc = o - _group_sum(o, gmat) * (1.0 / n)
        var = _group_sum(oc * oc, gmat) * (1.0 / n)
        y = oc * lax.rsqrt(var + RWKV_LN_EPS) * lnw_ref[...] + lnb_ref[...]
        y_ref[rows, :] = ((y + bonus_ref[rows, :].astype(F32)) * g_ref[rows, :].astype(F32)).astype(y_ref.dtype)
        return carry

    lax.fori_loop(0, (n_chunks * c) // post_rows, post, 0)


def _rwkv_scan(feats, lnw, lnb, *, batch, seq, width, total_rows):
    heads = width // RWKV_HEAD_DIM
    blk = _pick(width, (2 * LANES, LANES))
    hb = blk // RWKV_HEAD_DIM
    c = RWKV_CHUNK
    assert seq % c == 0
    n_chunks = seq // c
    tok = pl.BlockSpec((seq, blk), lambda b, h: (b, h))
    par = pl.BlockSpec((1, blk), lambda b, h: (0, h))
    return pl.pallas_call(
        functools.partial(_rwkv_scan_kernel, n_chunks=n_chunks, chunk_group=_pick(n_chunks, (4, 2, 1)),
                          post_rows=_pick(seq, (256, 128, 64))),
        grid=(batch, width // blk),
        in_specs=[tok] * 9 + [pl.BlockSpec((n_chunks * SUBLANES, blk), lambda b, h: (b, h)), par, par],
        out_specs=[tok, pl.BlockSpec((1, hb, RWKV_HEAD_DIM, RWKV_HEAD_DIM), lambda b, h: (b, h, 0, 0))],
        out_shape=[jax.ShapeDtypeStruct((total_rows, width), BF16),
                   jax.ShapeDtypeStruct((batch, heads, RWKV_HEAD_DIM, RWKV_HEAD_DIM), F32)],
        scratch_shapes=[pltpu.VMEM((seq, blk), BF16), pltpu.VMEM((seq, blk), F32),
                        pltpu.VMEM((n_chunks, blk // LANES, LANES, LANES), BF16),
                        pltpu.VMEM((n_chunks, blk // LANES, LANES, LANES), F32),
                        pltpu.VMEM((seq, blk), F32)],
        compiler_params=_params("parallel", "parallel"),
        name="rwkv_scan",
    )(*feats[:10], lnw, lnb)


def _rwkv_sample_kernel(r_ref, lw_ref, k_ref, v_ref, kk_ref, b_ref, g_ref, s_ref, rk_ref, lnw_ref, lnb_ref, *rest):
    y_ref, o_ref, v_scr, out_scr = rest[-4:]
    n = RWKV_HEAD_DIM
    tokens = r_ref.shape[0]

    def chan(ref):
        return ref[...].T

    def col(ref):
        return jnp.broadcast_to(ref[...], (tokens, LANES)).T

    r, k2, v, kk, b, g = chan(r_ref), chan(k_ref), chan(v_ref), chan(kk_ref), chan(b_ref), chan(g_ref)
    w = jnp.exp(chan(lw_ref))
    v_scr[:, 0, :] = v
    for hh in range(s_ref.shape[0]):
        ch = slice(hh * n, (hh + 1) * n)
        kk_h, w_h, b_h, k_h, r_h = kk[ch], w[ch], b[ch], k2[ch], r[ch]

        def body(vi, carry):
            s = s_ref[hh, vi]
            sa = -jnp.sum(s * kk_h, axis=0, keepdims=True)
            new = s * w_h + sa * b_h + v_scr[hh * n + vi] * k_h
            o_ref[hh, vi] = new
            out_scr[hh * n + vi] = jnp.sum(new * r_h, axis=0, keepdims=True)
            return carry

        lax.fori_loop(0, n, body, 0, unroll=4)
    o = out_scr[:, 0, :]
    rkk = r * k2 * col(rk_ref)
    yn, bonus = [], []
    for hh in range(s_ref.shape[0]):
        ch = slice(hh * n, (hh + 1) * n)
        oc = o[ch] - jnp.mean(o[ch], axis=0, keepdims=True)
        var = jnp.mean(oc * oc, axis=0, keepdims=True)
        yn.append(oc * lax.rsqrt(var + RWKV_LN_EPS))
        bonus.append(jnp.sum(rkk[ch], axis=0, keepdims=True) * v[ch])
    y = (jnp.concatenate(yn, axis=0) * col(lnw_ref) + col(lnb_ref) + jnp.concatenate(bonus, axis=0)) * g
    y_ref[...] = y.T.astype(y_ref.dtype)


def _rwkv_sample(feats, states_t, layer, y_buf, new_states, rk, lnw, lnb, *, row_start, width):
    heads, bs = states_t.shape[1], states_t.shape[4]
    hb = LANES // RWKV_HEAD_DIM
    assert row_start % bs == 0 and heads % hb == 0
    roff = row_start // bs
    tok = pl.BlockSpec((bs, LANES), lambda j: (0, j))
    par = pl.BlockSpec((1, LANES), lambda j: (0, j))
    st = pl.BlockSpec((None, hb, RWKV_HEAD_DIM, RWKV_HEAD_DIM, bs), lambda j: (layer, j, 0, 0, 0))
    in_specs = [tok] * 7 + [st] + [par] * 3
    alias_specs, alias_args, aliases = _in_place([y_buf, new_states], len(in_specs))
    return pl.pallas_call(
        _rwkv_sample_kernel,
        grid=(heads // hb,),
        in_specs=in_specs + alias_specs,
        out_specs=[pl.BlockSpec((bs, LANES), lambda j: (roff, j)), st],
        out_shape=[jax.ShapeDtypeStruct(y_buf.shape, y_buf.dtype),
                   jax.ShapeDtypeStruct(states_t.shape, states_t.dtype)],
        scratch_shapes=[pltpu.VMEM((LANES, 1, bs), F32), pltpu.VMEM((LANES, 1, bs), F32)],
        input_output_aliases=aliases,
        compiler_params=_params("parallel"),
        name="rwkv_sample",
    )(*feats, states_t, rk, lnw, lnb, *alias_args)


def _rope_tables(pos):
    half = RET_HEAD_DIM // 2
    inv_freq = ROPE_BASE ** (-jnp.arange(half, dtype=F32) / half)
    ang = pos[:, None] * inv_freq[None, :]
    cos, sin = jnp.cos(ang), jnp.sin(ang)
    return jnp.concatenate([cos, cos], axis=-1), jnp.concatenate([-sin, sin], axis=-1)


def _pad_rows(w, start, total):
    return jnp.zeros((total, w.shape[1]), BF16).at[start:start + w.shape[0]].set(_bf(w))


def kernel(x_prompt, x_sample, state_ret, state_rwkv, state_shift, w_in, w_in_vres, mu_shift, mu_shift_vres, ret_ln_w, rwkv_w0, rwkv_w_up, rwkv_a0, rwkv_a_up, rwkv_g_up, rwkv_v0, rwkv_v_up, rwkv_k_k, rwkv_k_a, rwkv_r_k, rwkv_ln_w, rwkv_ln_b, w_out, g_attn, g_ffn, w_gate, w_up, w_down, g_final):
    batch, seq, d = x_prompt.shape
    bs = x_sample.shape[0]
    assert x_sample.shape[1] == 1
    depth = w_in.shape[0]
    mp = batch * seq
    ret_w = d // 2
    rw_w = d - ret_w
    ret_cols = 4 * ret_w
    main_cols = ret_cols + 3 * rw_w
    r_decay, r_aaa, r_gate = rwkv_w_up.shape[1], rwkv_a_up.shape[1], rwkv_g_up.shape[1]
    r_mv = rwkv_v_up.shape[1]
    o_a, o_g = r_decay, r_decay + r_aaa
    o_v = o_g + r_gate
    assert o_v + r_mv <= LORA_PAD and w_in.shape[2] == main_cols + o_v
    ret_heads = ret_w // RET_HEAD_DIM

    log_gamma = jnp.log1p(-jnp.exp2(-5.0 - jnp.arange(ret_heads, dtype=F32)))
    lg = jnp.broadcast_to(log_gamma[:, None, None], (ret_heads, 1, LANES))
    cos_p, sin_p = _rope_tables(jnp.arange(seq, dtype=F32))
    cos_s, sin_s = _rope_tables(PAST_LEN + jnp.arange(1, dtype=F32))

    pad = LORA_PAD - o_v - r_mv
    w_in_t = jnp.transpose(w_in, (0, 2, 1))
    w_lora = jnp.concatenate(
        [w_in[:, :, main_cols:], jnp.concatenate([jnp.zeros((1, d, r_mv), F32), w_in_vres], axis=0),
         jnp.zeros((depth, d, pad), F32)], axis=-1)
    mu_all = jnp.concatenate(
        [mu_shift, jnp.concatenate([jnp.zeros((1, r_mv), F32), mu_shift_vres], axis=0), jnp.zeros((depth, pad), F32)],
        axis=-1)
    state_rwkv_t = jnp.transpose(state_rwkv, (0, 2, 3, 4, 1))

    h = jnp.concatenate([x_prompt.reshape(mp, d), x_sample.reshape(bs, d)], axis=0)
    vfirst_p = vfirst_s = None
    ret_p, rwkv_p, shift_p, shift_s = [], [], [], []
    ret_s = rwkv_s = None
    for l in range(depth):
        lp = {
            "mu": mu_all[l].reshape(1, -1),
            "w0": rwkv_w0[l].reshape(1, -1), "w_up": _pad_rows(rwkv_w_up[l], 0, LORA_PAD),
            "a0": rwkv_a0[l].reshape(1, -1), "a_up": _pad_rows(rwkv_a_up[l], o_a, LORA_PAD),
            "g_up": _pad_rows(rwkv_g_up[l], o_g, LORA_PAD),
            "k_k": rwkv_k_k[l].reshape(1, -1), "k_a": rwkv_k_a[l].reshape(1, -1),
            "r_k": rwkv_r_k[l].reshape(1, -1),
        }
        if l > 0:
            lp["v0"] = rwkv_v0[l - 1].reshape(1, -1)
            lp["v_up"] = _pad_rows(rwkv_v_up[l - 1], o_v, LORA_PAD)
        lnw, lnb = rwkv_ln_w[l].reshape(1, -1), rwkv_ln_b[l].reshape(1, -1)
        ret_lnw = ret_ln_w[l].reshape(1, -1)

        xn = _rmsnorm(h, g_attn[l], BF16)
        proj = _matmul([(xn, w_in_t, 0)], l, main_cols, F32, transposed=True, name="in_proj")
        proj_lora = _matmul([(xn, w_lora, 0)], l, LORA_PAD, F32, name="in_proj_lora")
        h_last = jnp.concatenate([h[seq - 1:mp:seq], h[mp:]], axis=0)
        xn_last = _rmsnorm(h_last, g_attn[l], F32)
        shift_p.append(xn_last[:batch])
        shift_s.append(xn_last[batch:])
        prev_tok = _bf(state_shift[l])
        prev_s = (_matmul([(prev_tok, w_in_t, 0)], l, 3 * rw_w, F32, col_start=ret_cols, transposed=True,
                          name="prev_proj"),
                  _matmul([(prev_tok, w_lora, 0)], l, LORA_PAD, F32, name="prev_proj_lora"))

        feats_p = _rwkv_prep(proj, proj_lora, None, lp, vfirst_p, row_start=0, rows=mp, seq=seq,
                             ret_cols=ret_cols, width=rw_w)
        feats_s = _rwkv_prep(proj, proj_lora, prev_s, lp, vfirst_s, row_start=mp, rows=bs, seq=1,
                             ret_cols=ret_cols, width=rw_w)
        if l == 0:
            vfirst_p, vfirst_s = feats_p[10], feats_s[3]

        y_ret, s_ret_p = _ret_prompt(proj, cos_p, sin_p, lg, ret_lnw, batch=batch, seq=seq, width=ret_w,
                                     total_rows=mp + bs)
        y_ret, ret_s = _ret_sample(proj, state_ret, l, y_ret, ret_s, cos_s, sin_s, lg, ret_lnw,
                                   row_start=mp, width=ret_w)
        y_rw, s_rw_p = _rwkv_scan(feats_p, lnw, lnb, batch=batch, seq=seq, width=rw_w, total_rows=mp + bs)
        y_rw, rwkv_s = _rwkv_sample(feats_s, state_rwkv_t, l, y_rw, rwkv_s, lp["r_k"], lnw, lnb,
                                    row_start=mp, width=rw_w)
        ret_p.append(s_ret_p)
        rwkv_p.append(s_rw_p)

        h = _matmul([(y_ret, w_out, 0), (y_rw, w_out, ret_w)], l, d, F32, res=h, name="out_proj")
        hn = _rmsnorm(h, g_ffn[l], BF16)
        act = _matmul([(hn, w_up, 0)], l, w_up.shape[2], BF16, gate=[w_gate], name="swiglu")
        h = _matmul([(act, w_down, 0)], l, d, F32, res=h, tm=_pick(mp + bs, FFN_DOWN_ROW_TILES), name="ffn_down")

    y_p = _rmsnorm(h, g_final, F32, row_start=0, rows=mp).reshape(batch, seq, d)
    y_s = _rmsnorm(h, g_final, F32, row_start=mp, rows=bs, tm=_pick(bs, (128, 64, 32, 16, 8))).reshape(bs, 1, d)
    return (y_p, y_s, jnp.stack(ret_p), jnp.stack(rwkv_p), jnp.stack(shift_p),
            ret_s, jnp.transpose(rwkv_s, (0, 4, 1, 2, 3)), jnp.stack(shift_s))
```

```python
import functools

import jax
import jax.numpy as jnp
from jax import lax
from jax.experimental import pallas as pl
from jax.experimental.pallas import tpu as pltpu

F32 = jnp.float32
BF16 = jnp.bfloat16

PAST_LEN = 16384
ROPE_BASE = 10000.0
RET_HEAD_DIM = 128
RET_CHUNK = 128
RWKV_HEAD_DIM = 64
RWKV_CHUNK = 64
NORM_EPS = 1e-6
RET_LN_EPS = 1e-5
RWKV_LN_EPS = 64e-5

LANES = 128
SUBLANES = 8
LORA_PAD = 512
VMEM_LIMIT_BYTES = 56 * 1024 * 1024
ROW_TILES = (1664, 832, 640, 512, 256, 128, 64, 32, 16)
FFN_DOWN_ROW_TILES = ROW_TILES[2:]
WIDE_COL_TILE = 1024
STEP_TOKENS = 8


def _params(*sem):
    return pltpu.CompilerParams(dimension_semantics=sem, vmem_limit_bytes=VMEM_LIMIT_BYTES)


def _pick(n, candidates):
    for c in candidates:
        if n % c == 0:
            return c
    raise ValueError(f"no tile in {candidates} divides {n}")


def _dot(a, b):
    return jnp.dot(a, b, preferred_element_type=F32)


def _dot_nt(a, b):
    return lax.dot_general(a, b, (((1,), (1,)), ((), ())), preferred_element_type=F32)


def _dot_tn(a, b):
    return lax.dot_general(a, b, (((0,), (0,)), ((), ())), preferred_element_type=F32)


def _bf(x):
    return x.astype(BF16)


def _split2(x):
    hi = _bf(x)
    return hi, _bf(x - hi.astype(F32))


def _rmsnorm_kernel(x_ref, g_ref, o_ref):
    x = x_ref[...]
    y = x * lax.rsqrt(jnp.mean(x * x, axis=-1, keepdims=True) + NORM_EPS)
    o_ref[...] = (y * g_ref[...]).astype(o_ref.dtype)


def _rmsnorm(x, g, out_dtype, *, row_start=0, rows=None, tm=None):
    m, d = x.shape
    rows = m if rows is None else rows
    if tm is None:
        tm = rows if rows <= 640 else _pick(rows, (640, 512, 256, 128, 64, 32, 16))
    assert row_start % tm == 0 and rows % tm == 0
    off = row_start // tm
    return pl.pallas_call(
        _rmsnorm_kernel,
        grid=(rows // tm,),
        in_specs=[pl.BlockSpec((tm, d), lambda i: (i + off, 0)), pl.BlockSpec((1, d), lambda i: (0, 0))],
        out_specs=pl.BlockSpec((tm, d), lambda i: (i, 0)),
        out_shape=jax.ShapeDtypeStruct((rows, d), out_dtype),
        compiler_params=_params("parallel"),
        name="rmsnorm",
    )(x, g.reshape(1, d))


def _mm_kernel(*refs, n_pairs, has_res, transposed, swiglu, row_splits):
    n_w = 2 * n_pairs if swiglu else n_pairs
    a_refs, w_refs = refs[:n_pairs], refs[n_pairs:n_pairs + n_w]
    w_scr = refs[len(refs) - n_w:]
    o_ref = refs[len(refs) - n_w - 1]

    @pl.when(pl.program_id(1) == 0)
    def _():
        for w_ref, scr in zip(w_refs, w_scr):
            scr[...] = w_ref[...].astype(BF16)

    def product(scrs, rows):
        acc = None
        for a_ref, scr in zip(a_refs, scrs):
            d = _dot_nt(a_ref[rows, :], scr[...]) if transposed else _dot(a_ref[rows, :], scr[...])
            acc = d if acc is None else acc + d
        return acc

    tm = o_ref.shape[0]
    sub = tm // row_splits
    for r in range(row_splits):
        rows = slice(r * sub, (r + 1) * sub)
        acc = product(w_scr[:n_pairs], rows)
        if swiglu:
            acc = acc * jax.nn.sigmoid(acc) * product(w_scr[n_pairs:], rows)
        if has_res:
            acc = acc + refs[n_pairs + n_w][rows, :]
        o_ref[rows, :] = acc.astype(o_ref.dtype)


def _matmul(pairs, layer, n, out_dtype, *, res=None, col_start=0, transposed=False, gate=None, tm=None, tn=512,
            name="matmul"):
    m = pairs[0][0].shape[0]
    tm = _pick(m, ROW_TILES) if tm is None else tm
    assert col_start % tn == 0 and n % tn == 0 and m % tm == 0
    coff = col_start // tn
    in_specs, args, scratch = [], [], []
    for a, _, _ in pairs:
        in_specs.append(pl.BlockSpec((tm, a.shape[1]), lambda j, i: (i, 0)))
        args.append(a)
    weights = [(a.shape[1], g, k_start) for (a, _, k_start), g in zip(pairs, gate or [])]
    weights += [(a.shape[1], w, k_start) for a, w, k_start in pairs]
    for k, w, k_start in weights:
        assert k_start % k == 0
        kb = k_start // k
        if transposed:
            in_specs.append(pl.BlockSpec((None, tn, k), lambda j, i, kb=kb: (layer, j + coff, kb)))
            scratch.append(pltpu.VMEM((tn, k), BF16))
        else:
            in_specs.append(pl.BlockSpec((None, k, tn), lambda j, i, kb=kb: (layer, kb, j + coff)))
            scratch.append(pltpu.VMEM((k, tn), BF16))
        args.append(w)
    if res is not None:
        in_specs.append(pl.BlockSpec((tm, tn), lambda j, i: (i, j)))
        args.append(res)
    return pl.pallas_call(
        functools.partial(_mm_kernel, n_pairs=len(pairs), has_res=res is not None, transposed=transposed,
                          swiglu=gate is not None, row_splits=_pick(tm // 16, (4, 2, 1))),
        grid=(n // tn, m // tm),
        in_specs=in_specs,
        out_specs=pl.BlockSpec((tm, tn), lambda j, i: (i, j)),
        out_shape=jax.ShapeDtypeStruct((m, n), out_dtype),
        scratch_shapes=scratch,
        compiler_params=_params("parallel", "arbitrary"),
        name=name,
    )(*args)


def _rope(x, cos, sin_signed):
    return x * cos + pltpu.roll(x, RET_HEAD_DIM // 2, 1) * sin_signed


def _head_norm(y, eps):
    mu = jnp.mean(y, axis=-1, keepdims=True)
    yc = y - mu
    var = jnp.mean(yc * yc, axis=-1, keepdims=True)
    return yc * lax.rsqrt(var + eps)


def _ret_prompt_kernel(q_ref, k_ref, v_ref, g_ref, cos_ref, sin_ref, lg_ref, lnw_ref, y_ref, s_ref, *,
                       n_chunks, group):
    c, d = RET_CHUNK, RET_HEAD_DIM
    lg = lg_ref[0]
    row = lax.broadcasted_iota(jnp.int32, (c, c), 0)
    col = lax.broadcasted_iota(jnp.int32, (c, c), 1)
    diff = (row - col).astype(F32)
    decay_in = jnp.where(diff >= 0, jnp.exp(lg * jnp.maximum(diff, 0.0)), 0.0)
    ridx = lax.broadcasted_iota(jnp.int32, (c, d), 0).astype(F32)
    tail = jnp.exp((c - 1 - ridx) * lg)
    head = jnp.exp((ridx + 1) * lg)
    chunk_decay = jnp.exp(c * lg)
    lnw = lnw_ref[...]

    def body(gi, s):
        rows = [pl.ds(pl.multiple_of((gi * group + j) * c, c), c) for j in range(group)]
        cs = [(cos_ref[r, :], sin_ref[r, :]) for r in rows]
        q = [_rope(q_ref[r, :], co, si) for r, (co, si) in zip(rows, cs)]
        k = [_rope(k_ref[r, :], co, si) * (d ** -0.5) for r, (co, si) in zip(rows, cs)]
        vb = [_bf(v_ref[r, :]) for r in rows]
        scores = [_dot_nt(_bf(qj), _bf(kj)) * decay_in for qj, kj in zip(q, k)]
        kv = [_dot_tn(_bf(kj * tail), vj) for kj, vj in zip(k, vb)]
        states = [s]
        for j in range(group):
            states.append(states[j] * chunk_decay + kv[j])
        y = [_dot(_bf(sc), vj) + _dot(_bf(qj * head), _bf(sj))
             for sc, vj, qj, sj in zip(scores, vb, q, states)]
        for r, yj in zip(rows, y):
            g = g_ref[r, :]
            y_ref[r, :] = (g * jax.nn.sigmoid(g) * (_head_norm(yj, RET_LN_EPS) * lnw)).astype(y_ref.dtype)
        return states[group]

    s_ref[0, 0] = lax.fori_loop(0, n_chunks // group, body, jnp.zeros((d, d), F32))


def _ret_prompt(proj, cos, sin, lg, ln_w, *, batch, seq, width, total_rows):
    heads = width // RET_HEAD_DIM
    assert seq % RET_CHUNK == 0 and RET_CHUNK == LANES
    n_chunks = seq // RET_CHUNK

    def col(which):
        return pl.BlockSpec((seq, RET_HEAD_DIM), lambda b, h: (b, which * heads + h))

    return pl.pallas_call(
        functools.partial(_ret_prompt_kernel, n_chunks=n_chunks, group=_pick(n_chunks, (8, 4, 2, 1))),
        grid=(batch, heads),
        in_specs=[col(0), col(1), col(2), col(3),
                  pl.BlockSpec((seq, RET_HEAD_DIM), lambda b, h: (0, 0)),
                  pl.BlockSpec((seq, RET_HEAD_DIM), lambda b, h: (0, 0)),
                  pl.BlockSpec((1, 1, LANES), lambda b, h: (h, 0, 0)),
                  pl.BlockSpec((1, RET_HEAD_DIM), lambda b, h: (0, h))],
        out_specs=[pl.BlockSpec((seq, RET_HEAD_DIM), lambda b, h: (b, h)),
                   pl.BlockSpec((1, 1, RET_HEAD_DIM, RET_HEAD_DIM), lambda b, h: (b, h, 0, 0))],
        out_shape=[jax.ShapeDtypeStruct((total_rows, width), BF16),
                   jax.ShapeDtypeStruct((batch, heads, RET_HEAD_DIM, RET_HEAD_DIM), F32)],
        compiler_params=_params("parallel", "parallel"),
        name="ret_prompt",
    )(proj, proj, proj, proj, cos, sin, lg, ln_w)


def _ret_sample_kernel(q_ref, k_ref, v_ref, g_ref, s_ref, cos_ref, sin_ref, lg_ref, lnw_ref, *rest):
    y_ref, o_ref, q_scr, kp_scr, v_scr, y_scr = rest[-6:]
    d = RET_HEAD_DIM
    tt, width = q_ref.shape
    heads = width // d
    cos, sin = cos_ref[...], sin_ref[...]
    v_scr[:, 0, :] = v_ref[...]
    for h in range(heads):
        ln = slice(h * d, (h + 1) * d)
        q_scr[:, 0, ln] = _rope(q_ref[:, ln], cos, sin)
        k = _rope(k_ref[:, ln], cos, sin) * (d ** -0.5)
        k_hi = _bf(k).astype(F32)
        kp_scr[:, 0, 2 * h * d:(2 * h + 1) * d] = k_hi
        kp_scr[:, 0, (2 * h + 1) * d:(2 * h + 2) * d] = k - k_hi
    row = lax.broadcasted_iota(jnp.int32, (d, 2 * d), 0)
    col = lax.broadcasted_iota(jnp.int32, (d, 2 * d), 1)
    eye2 = jnp.where((col == row) | (col == row + d), 1.0, 0.0).astype(BF16)

    def body(ti, carry):
        th = [(2 * ti + j, h) for j in range(2) for h in range(heads)]
        lns = [slice(h * d, (h + 1) * d) for _, h in th]
        kp = [_bf(jnp.broadcast_to(kp_scr[t, :, 2 * h * d:(2 * h + 2) * d], (d, 2 * d))) for t, h in th]
        k_col = [_dot_nt(eye2, x) for x in kp]
        s_new = [s_ref[t, h] * jnp.exp(lg_ref[h]) + kc * v_scr[t, :, ln] for (t, h), kc, ln in zip(th, k_col, lns)]
        for (t, h), s in zip(th, s_new):
            o_ref[t, h] = s
        y = [_dot(_bf(jnp.broadcast_to(q_scr[t, :, ln], (SUBLANES, d))), _bf(s))
             for (t, h), s, ln in zip(th, s_new, lns)]
        for (t, h), yi, ln in zip(th, y, lns):
            y_scr[t, :, ln] = yi[0:1]
        return carry

    assert tt % 2 == 0
    lax.fori_loop(0, tt // 2, body, 0)
    for h in range(heads):
        ln = slice(h * d, (h + 1) * d)
        g = g_ref[:, ln]
        yn = _head_norm(y_scr[:, 0, ln], RET_LN_EPS) * lnw_ref[:, ln]
        y_ref[:, ln] = (g * jax.nn.sigmoid(g) * yn).astype(y_ref.dtype)


def _in_place(bufs, n_inputs):
    specs, args, aliases = [], [], {}
    for out_idx, buf in enumerate(bufs):
        if buf is not None:
            aliases[n_inputs + len(args)] = out_idx
            specs.append(pl.BlockSpec(memory_space=pl.ANY))
            args.append(buf)
    return specs, args, aliases


def _ret_sample(proj, states, layer, y_buf, new_states, cos, sin, lg, ln_w, *, row_start, width):
    bs, heads = states.shape[1], states.shape[2]
    tt = STEP_TOKENS
    assert row_start % tt == 0 and bs % tt == 0
    roff = row_start // tt

    def col(which):
        return pl.BlockSpec((tt, width), lambda i: (i + roff, which))

    st = pl.BlockSpec((None, tt, heads, RET_HEAD_DIM, RET_HEAD_DIM), lambda i: (layer, i, 0, 0, 0))
    in_specs = [col(0), col(1), col(2), col(3), st,
                pl.BlockSpec((1, RET_HEAD_DIM), lambda i: (0, 0)),
                pl.BlockSpec((1, RET_HEAD_DIM), lambda i: (0, 0)),
                pl.BlockSpec((heads, 1, LANES), lambda i: (0, 0, 0)),
                pl.BlockSpec((1, width), lambda i: (0, 0))]
    alias_specs, alias_args, aliases = _in_place([y_buf, new_states], len(in_specs))
    return pl.pallas_call(
        _ret_sample_kernel,
        grid=(bs // tt,),
        in_specs=in_specs + alias_specs,
        out_specs=[pl.BlockSpec((tt, width), lambda i: (i + roff, 0)), st],
        out_shape=[jax.ShapeDtypeStruct(y_buf.shape, y_buf.dtype), jax.ShapeDtypeStruct(states.shape, states.dtype)],
        scratch_shapes=[pltpu.VMEM((tt, 1, width), F32), pltpu.VMEM((tt, 1, 2 * width), F32),
                        pltpu.VMEM((tt, 1, width), F32), pltpu.VMEM((tt, 1, width), F32)],
        input_output_aliases=aliases,
        compiler_params=_params("parallel"),
        name="ret_sample",
    )(proj, proj, proj, proj, states, cos, sin, lg, ln_w, *alias_args)


def _softplus(x):
    return jnp.maximum(x, 0.0) + jnp.log1p(jnp.exp(-jnp.abs(x)))


def _group_matrix():
    r = lax.broadcasted_iota(jnp.int32, (LANES, LANES), 0) // RWKV_HEAD_DIM
    c = lax.broadcasted_iota(jnp.int32, (LANES, LANES), 1) // RWKV_HEAD_DIM
    return jnp.where(r == c, 1.0, 0.0).astype(BF16)


def _group_sum(x, gmat):
    outs = []
    for j in range(x.shape[1] // LANES):
        hi, lo = _split2(x[:, j * LANES:(j + 1) * LANES])
        outs.append(_dot(hi, gmat) + _dot(lo, gmat))
    return outs[0] if len(outs) == 1 else jnp.concatenate(outs, axis=1)


def _cumsum_rows(x, tri):
    hi = _bf(x)
    r1 = x - hi.astype(F32)
    mid = _bf(r1)
    lo = _bf(r1 - mid.astype(F32))
    return _dot(tri, hi) + _dot(tri, mid) + _dot(tri, lo)


def _rwkv_prep_kernel(*refs, seq_mode, has_vres, tiles_per_seq, width):
    it = iter(refs)
    cur = [next(it) for _ in range(4)]
    prev = [next(it) for _ in range(4)]
    mu_ref, w0_ref, wup_ref, a0_ref, aup_ref, gup_ref, kk_ref, ka_ref, rk_ref = (next(it) for _ in range(9))
    if has_vres:
        v0_ref, vup_ref, vfirst_ref = (next(it) for _ in range(3))
    outs = list(it)
    i = pl.program_id(0)

    def shifted(idx, mu):
        p = cur[idx][...]
        if seq_mode:
            carry = jnp.where(i % tiles_per_seq == 0, 0.0, prev[idx][SUBLANES - 1:SUBLANES, :])
            rowi = lax.broadcasted_iota(jnp.int32, p.shape, 0)
            p_prev = jnp.where(rowi == 0, carry, pltpu.roll(p, 1, 0))
        else:
            p_prev = prev[idx][...]
        return p + (p_prev - p) * mu

    w = width
    r = shifted(0, mu_ref[:, 0:w])
    k = shifted(1, mu_ref[:, w:2 * w])
    v = shifted(2, mu_ref[:, 2 * w:3 * w])
    lo = shifted(3, mu_ref[:, 3 * w:])
    lob = _bf(lo)
    w_raw = -_softplus(-(w0_ref[...] + _dot(_bf(jnp.tanh(lo)), wup_ref[...]))) - 0.5
    lw = -jnp.exp(w_raw)
    a = jax.nn.sigmoid(a0_ref[...] + _dot(lob, aup_ref[...]))
    g = _dot(_bf(jax.nn.sigmoid(lo)), gup_ref[...])
    if has_vres:
        v = v + (vfirst_ref[...] - v) * jax.nn.sigmoid(v0_ref[...] + _dot(lob, vup_ref[...]))
    k2 = k * (1.0 + (a - 1.0) * ka_ref[...])
    kk = k * kk_ref[...]
    gmat = _group_matrix()
    kk = kk / jnp.maximum(jnp.sqrt(_group_sum(kk * kk, gmat)), 1e-12)
    b = kk * a
    if not seq_mode:
        for o_ref, val in zip(outs, (r, lw, k2, v, kk, b, g)):
            o_ref[...] = val
        return

    tp, c = r.shape[0], RWKV_CHUNK
    row = lax.broadcasted_iota(jnp.int32, (tp, tp), 0)
    col = lax.broadcasted_iota(jnp.int32, (tp, tp), 1)
    tri = jnp.where(row >= col, jnp.where(row // c == col // c, 1.0, 0.0), 0.0).astype(BF16)
    cum = _cumsum_rows(lw, tri)
    last = [cum[j * c + c - 1:j * c + c, :] for j in range(tp // c)]
    tot = jnp.concatenate([jnp.broadcast_to(x, (c, w)) for x in last], axis=0)
    e_in = jnp.exp(-cum)
    e_out = jnp.exp(tot - cum)
    bonus = _group_sum(r * k2 * rk_ref[...], gmat) * v
    vals = (kk * jnp.exp(cum - lw), r * jnp.exp(cum), b * e_in, k2 * e_in, b * e_out, k2 * e_out, v, bonus, g)
    for o_ref, val in zip(outs, vals):
        o_ref[...] = val.astype(o_ref.dtype)
    outs[9][...] = jnp.concatenate([jnp.broadcast_to(jnp.exp(x), (SUBLANES, w)) for x in last], axis=0)
    if not has_vres:
        outs[10][...] = v


def _rwkv_prep(proj, proj_lora, prev, lp, vfirst, *, row_start, rows, seq, ret_cols, width):
    seq_mode = prev is None
    has_vres = vfirst is not None
    tp = _pick(seq, (256, 128, 64)) if seq_mode else _pick(rows, (128, 64, 32, 16, 8))
    assert row_start % tp == 0 and ret_cols % width == 0
    roff = row_start // tp
    cb = ret_cols // width
    cur = ((proj, width, cb), (proj, width, cb + 1), (proj, width, cb + 2), (proj_lora, LORA_PAD, 0))
    in_specs, args = [], []
    for arr, wd, blk in cur:
        in_specs.append(pl.BlockSpec((tp, wd), lambda i, blk=blk: (i + roff, blk)))
        args.append(arr)
    if seq_mode:
        for arr, wd, blk in cur:
            in_specs.append(pl.BlockSpec(
                (SUBLANES, wd), lambda i, blk=blk: (jnp.maximum((i + roff) * (tp // SUBLANES) - 1, 0), blk)))
            args.append(arr)
    else:
        for arr, wd, blk in ((prev[0], width, 0), (prev[0], width, 1), (prev[0], width, 2), (prev[1], LORA_PAD, 0)):
            in_specs.append(pl.BlockSpec((tp, wd), lambda i, blk=blk: (i, blk)))
            args.append(arr)

    def full(x):
        return pl.BlockSpec(x.shape, lambda i: (0, 0))

    names = ["mu", "w0", "w_up", "a0", "a_up", "g_up", "k_k", "k_a", "r_k"] + (["v0", "v_up"] if has_vres else [])
    for nm in names:
        in_specs.append(full(lp[nm]))
        args.append(lp[nm])
    if has_vres:
        in_specs.append(pl.BlockSpec((tp, width), lambda i: (i, 0)))
        args.append(vfirst)
    tok = pl.BlockSpec((tp, width), lambda i: (i, 0))
    if seq_mode:
        assert tp % RWKV_CHUNK == 0
        per_chunk = pl.BlockSpec((tp // RWKV_CHUNK * SUBLANES, width), lambda i: (i, 0))
        out_specs = [tok] * 9 + [per_chunk] + ([] if has_vres else [tok])
        out_shape = ([jax.ShapeDtypeStruct((rows, width), BF16)] * 9
                     + [jax.ShapeDtypeStruct((rows // RWKV_CHUNK * SUBLANES, width), F32)]
                     + ([] if has_vres else [jax.ShapeDtypeStruct((rows, width), F32)]))
    else:
        out_specs = [tok] * 7
        out_shape = [jax.ShapeDtypeStruct((rows, width), F32)] * 7
    return pl.pallas_call(
        functools.partial(_rwkv_prep_kernel, seq_mode=seq_mode, has_vres=has_vres,
                          tiles_per_seq=max(seq // tp, 1), width=width),
        grid=(rows // tp,),
        in_specs=in_specs,
        out_specs=out_specs,
        out_shape=out_shape,
        compiler_params=_params("parallel"),
        name="rwkv_prep",
    )(*args)


def _rwkv_scan_kernel(khat_ref, rhat_ref, btil_ref, ktil_ref, bbar_ref, kbar_ref, v_ref, bonus_ref, g_ref,
                      dtot_ref, lnw_ref, lnb_ref, y_ref, s_ref,
                      r_scr, o0_scr, m_scr, n_scr, o_scr, *, n_chunks, chunk_group, post_rows):
    c, n = RWKV_CHUNK, RWKV_HEAD_DIM
    pairs = khat_ref.shape[1] // LANES
    row = lax.broadcasted_iota(jnp.int32, (c, c), 0)
    col = lax.broadcasted_iota(jnp.int32, (c, c), 1)
    strict, incl = row > col, row >= col
    eye = jnp.where(row == col, 1.0, 0.0)
    first = lax.broadcasted_iota(jnp.int32, (c, LANES), 1) < n
    first_b = jnp.where(first, 1.0, 0.0).astype(BF16)
    second_b = jnp.where(first, 0.0, 1.0).astype(BF16)
    vrow = lax.broadcasted_iota(jnp.int32, (LANES, LANES), 0) < n
    kcol = lax.broadcasted_iota(jnp.int32, (LANES, LANES), 1) < n
    same_head = vrow == kcol

    def lanes(p):
        return slice(p * LANES, (p + 1) * LANES)

    def pass1(gi, carry):
        inst = [(gi * chunk_group + cj, p) for cj in range(chunk_group) for p in range(pairs)]
        rows = [pl.ds(pl.multiple_of(ci * c, c), c) for ci, _ in inst]
        at = [(rows[i], lanes(p)) for i, (_, p) in enumerate(inst)]
        khat = [khat_ref[a] for a in at]
        rhat = [rhat_ref[a] for a in at]
        v2 = [v_ref[a] for a in at]
        x = [jnp.concatenate([kh * first_b, kh * second_b, rh * first_b, rh * second_b], axis=0)
             for kh, rh in zip(khat, rhat)]
        ab = [_dot_nt(xi, btil_ref[a]) for xi, a in zip(x, at)]
        ak = [_dot_nt(xi, ktil_ref[a]) for xi, a in zip(x, at)]
        heads2 = [(i, hh) for i in range(len(inst)) for hh in range(2)]
        pw = [-jnp.where(strict, ab[i][hh * c:(hh + 1) * c], 0.0) for i, hh in heads2]
        a_k = [_bf(jnp.where(strict, ak[i][hh * c:(hh + 1) * c], 0.0)) for i, hh in heads2]
        a_rk = [_bf(jnp.where(incl, ak[i][(2 + hh) * c:(3 + hh) * c], 0.0)) for i, hh in heads2]
        a_rb = [_bf(jnp.where(incl, ab[i][(2 + hh) * c:(3 + hh) * c], 0.0)) for i, hh in heads2]
        akv = [_dot(a_k[j], v2[i]) for j, (i, _) in enumerate(heads2)]
        arkv = [_dot(a_rk[j], v2[i]) for j, (i, _) in enumerate(heads2)]
        akv_b = [_bf(jnp.where(first, akv[2 * i], akv[2 * i + 1])) for i in range(len(inst))]
        tinv = [eye + q for q in pw]
        for _ in range(c.bit_length() - 2):
            pwb = [_bf(q) for q in pw]
            pw = [_dot(q, q) for q in pwb]
            tinv = [t + _dot(_bf(t), _bf(q)) for t, q in zip(tinv, pw)]
        tinv = [_bf(t) for t in tinv]
        wt = [_dot(tinv[j], khat[i]) for j, (i, _) in enumerate(heads2)]
        u0 = [_dot(tinv[j], akv_b[i]) for j, (i, _) in enumerate(heads2)]
        wt_b = [_bf(jnp.where(first, wt[2 * i], wt[2 * i + 1])) for i in range(len(inst))]
        u0_b = [_bf(jnp.where(first, u0[2 * i], u0[2 * i + 1])) for i in range(len(inst))]
        bbar = [bbar_ref[a] for a in at]
        m = [_dot_tn(wt_b[i], bbar[i]) for i in range(len(inst))]
        nn = [_dot_tn(jnp.concatenate([v2[i], -u0_b[i]], axis=0),
                      jnp.concatenate([kbar_ref[at[i]], bbar[i]], axis=0)) for i in range(len(inst))]
        rw = [_dot(a_rb[j], wt_b[i]) for j, (i, _) in enumerate(heads2)]
        ru = [_dot(a_rb[j], u0_b[i]) for j, (i, _) in enumerate(heads2)]
        for i, (ci, p) in enumerate(inst):
            m_scr[ci, p] = _bf(jnp.where(same_head, -m[i], 0.0))
            n_scr[ci, p] = jnp.where(same_head, nn[i], 0.0)
            r_scr[at[i]] = _bf(rhat[i].astype(F32) - jnp.where(first, rw[2 * i], rw[2 * i + 1]))
            o0_scr[at[i]] = jnp.where(first, arkv[2 * i] - ru[2 * i], arkv[2 * i + 1] - ru[2 * i + 1])
        return carry

    lax.fori_loop(0, n_chunks // chunk_group, pass1, 0)

    def pass2(ci, states):
        rows = pl.ds(pl.multiple_of(ci * c, c), c)
        drows = pl.ds(pl.multiple_of(ci * SUBLANES, SUBLANES), SUBLANES)
        sb = [_bf(s) for s in states]
        ds = [_dot(sb[p], m_scr[ci, p]) for p in range(pairs)]
        for p in range(pairs):
            at = (rows, lanes(p))
            o_scr[at] = _dot_nt(r_scr[at], sb[p]) + o0_scr[at]
        return tuple(states[p] * dtot_ref[drows, lanes(p)][0:1] + ds[p] + n_scr[ci, p] for p in range(pairs))

    final = lax.fori_loop(0, n_chunks, pass2, tuple(jnp.zeros((LANES, LANES), F32) for _ in range(pairs)))
    for p in range(pairs):
        s_ref[0, 2 * p] = final[p][:n, :n]
        s_ref[0, 2 * p + 1] = final[p][n:, n:]

    gmat = _group_matrix()

    def post(ti, carry):
        rows = pl.ds(pl.multiple_of(ti * post_rows, post_rows), post_rows)
        o = o_scr[rows, :]
        oc = o - _group_sum(o, gmat) * (1.0 / n)
        var = _group_sum(oc * oc, gmat) * (1.0 / n)
        y = oc * lax.rsqrt(var + RWKV_LN_EPS) * lnw_ref[...] + lnb_ref[...]
        y_ref[rows, :] = ((y + bonus_ref[rows, :].astype(F32)) * g_ref[rows, :].astype(F32)).astype(y_ref.dtype)
        return carry

    lax.fori_loop(0, (n_chunks * c) // post_rows, post, 0)


def _rwkv_scan(feats, lnw, lnb, *, batch, seq, width, total_rows):
    heads = width // RWKV_HEAD_DIM
    blk = _pick(width, (2 * LANES, LANES))
    hb = blk // RWKV_HEAD_DIM
    c = RWKV_CHUNK
    assert seq % c == 0
    n_chunks = seq // c
    tok = pl.BlockSpec((seq, blk), lambda b, h: (b, h))
    par = pl.BlockSpec((1, blk), lambda b, h: (0, h))
    return pl.pallas_call(
        functools.partial(_rwkv_scan_kernel, n_chunks=n_chunks, chunk_group=_pick(n_chunks, (8, 4, 2, 1)),
                          post_rows=_pick(seq, (256, 128, 64))),
        grid=(batch, width // blk),
        in_specs=[tok] * 9 + [pl.BlockSpec((n_chunks * SUBLANES, blk), lambda b, h: (b, h)), par, par],
        out_specs=[tok, pl.BlockSpec((1, hb, RWKV_HEAD_DIM, RWKV_HEAD_DIM), lambda b, h: (b, h, 0, 0))],
        out_shape=[jax.ShapeDtypeStruct((total_rows, width), BF16),
                   jax.ShapeDtypeStruct((batch, heads, RWKV_HEAD_DIM, RWKV_HEAD_DIM), F32)],
        scratch_shapes=[pltpu.VMEM((seq, blk), BF16), pltpu.VMEM((seq, blk), F32),
                        pltpu.VMEM((n_chunks, blk // LANES, LANES, LANES), BF16),
                        pltpu.VMEM((n_chunks, blk // LANES, LANES, LANES), F32),
                        pltpu.VMEM((seq, blk), F32)],
        compiler_params=_params("parallel", "parallel"),
        name="rwkv_scan",
    )(*feats[:10], lnw, lnb)


def _rwkv_sample_kernel(r_ref, lw_ref, k_ref, v_ref, kk_ref, b_ref, g_ref, s_ref, rk_ref, lnw_ref, lnb_ref, *rest):
    y_ref, o_ref, v_scr, out_scr = rest[-4:]
    n = RWKV_HEAD_DIM
    tokens = r_ref.shape[0]

    def chan(ref):
        return ref[...].T

    def col(ref):
        return jnp.broadcast_to(ref[...], (tokens, LANES)).T

    r, k2, v, kk, b, g = chan(r_ref), chan(k_ref), chan(v_ref), chan(kk_ref), chan(b_ref), chan(g_ref)
    w = jnp.exp(chan(lw_ref))
    v_scr[:, 0, :] = v
    for hh in range(s_ref.shape[0]):
        ch = slice(hh * n, (hh + 1) * n)
        kk_h, w_h, b_h, k_h, r_h = kk[ch], w[ch], b[ch], k2[ch], r[ch]

        def body(vi, carry):
            s = s_ref[hh, vi]
            sa = -jnp.sum(s * kk_h, axis=0, keepdims=True)
            new = s * w_h + sa * b_h + v_scr[hh * n + vi] * k_h
            o_ref[hh, vi] = new
            out_scr[hh * n + vi] = jnp.sum(new * r_h, axis=0, keepdims=True)
            return carry

        lax.fori_loop(0, n, body, 0, unroll=4)
    o = out_scr[:, 0, :]
    rkk = r * k2 * col(rk_ref)
    yn, bonus = [], []
    for hh in range(s_ref.shape[0]):
        ch = slice(hh * n, (hh + 1) * n)
        oc = o[ch] - jnp.mean(o[ch], axis=0, keepdims=True)
        var = jnp.mean(oc * oc, axis=0, keepdims=True)
        yn.append(oc * lax.rsqrt(var + RWKV_LN_EPS))
        bonus.append(jnp.sum(rkk[ch], axis=0, keepdims=True) * v[ch])
    y = (jnp.concatenate(yn, axis=0) * col(lnw_ref) + col(lnb_ref) + jnp.concatenate(bonus, axis=0)) * g
    y_ref[...] = y.T.astype(y_ref.dtype)


def _rwkv_sample(feats, states_t, layer, y_buf, new_states, rk, lnw, lnb, *, row_start, width):
    heads, bs = states_t.shape[1], states_t.shape[4]
    hb = LANES // RWKV_HEAD_DIM
    assert row_start % bs == 0 and heads % hb == 0
    roff = row_start // bs
    tok = pl.BlockSpec((bs, LANES), lambda j: (0, j))
    par = pl.BlockSpec((1, LANES), lambda j: (0, j))
    st = pl.BlockSpec((None, hb, RWKV_HEAD_DIM, RWKV_HEAD_DIM, bs), lambda j: (layer, j, 0, 0, 0))
    in_specs = [tok] * 7 + [st] + [par] * 3
    alias_specs, alias_args, aliases = _in_place([y_buf, new_states], len(in_specs))
    return pl.pallas_call(
        _rwkv_sample_kernel,
        grid=(heads // hb,),
        in_specs=in_specs + alias_specs,
        out_specs=[pl.BlockSpec((bs, LANES), lambda j: (roff, j)), st],
        out_shape=[jax.ShapeDtypeStruct(y_buf.shape, y_buf.dtype),
                   jax.ShapeDtypeStruct(states_t.shape, states_t.dtype)],
        scratch_shapes=[pltpu.VMEM((LANES, 1, bs), F32), pltpu.VMEM((LANES, 1, bs), F32)],
        input_output_aliases=aliases,
        compiler_params=_params("parallel"),
        name="rwkv_sample",
    )(*feats, states_t, rk, lnw, lnb, *alias_args)


def _rope_tables(pos):
    half = RET_HEAD_DIM // 2
    inv_freq = ROPE_BASE ** (-jnp.arange(half, dtype=F32) / half)
    ang = pos[:, None] * inv_freq[None, :]
    cos, sin = jnp.cos(ang), jnp.sin(ang)
    return jnp.concatenate([cos, cos], axis=-1), jnp.concatenate([-sin, sin], axis=-1)


def _pad_rows(w, start, total):
    return jnp.zeros((total, w.shape[1]), BF16).at[start:start + w.shape[0]].set(_bf(w))


def kernel(x_prompt, x_sample, state_ret, state_rwkv, state_shift, w_in, w_in_vres, mu_shift, mu_shift_vres, ret_ln_w, rwkv_w0, rwkv_w_up, rwkv_a0, rwkv_a_up, rwkv_g_up, rwkv_v0, rwkv_v_up, rwkv_k_k, rwkv_k_a, rwkv_r_k, rwkv_ln_w, rwkv_ln_b, w_out, g_attn, g_ffn, w_gate, w_up, w_down, g_final):
    batch, seq, d = x_prompt.shape
    bs = x_sample.shape[0]
    assert x_sample.shape[1] == 1
    depth = w_in.shape[0]
    mp = batch * seq
    ret_w = d // 2
    rw_w = d - ret_w
    ret_cols = 4 * ret_w
    main_cols = ret_cols + 3 * rw_w
    r_decay, r_aaa, r_gate = rwkv_w_up.shape[1], rwkv_a_up.shape[1], rwkv_g_up.shape[1]
    r_mv = rwkv_v_up.shape[1]
    o_a, o_g = r_decay, r_decay + r_aaa
    o_v = o_g + r_gate
    assert o_v + r_mv <= LORA_PAD and w_in.shape[2] == main_cols + o_v
    ret_heads = ret_w // RET_HEAD_DIM

    log_gamma = jnp.log1p(-jnp.exp2(-5.0 - jnp.arange(ret_heads, dtype=F32)))
    lg = jnp.broadcast_to(log_gamma[:, None, None], (ret_heads, 1, LANES))
    cos_p, sin_p = _rope_tables(jnp.arange(seq, dtype=F32))
    cos_s, sin_s = _rope_tables(PAST_LEN + jnp.arange(1, dtype=F32))

    pad = LORA_PAD - o_v - r_mv
    w_in_t = jnp.transpose(w_in, (0, 2, 1))
    w_lora = jnp.concatenate(
        [w_in[:, :, main_cols:], jnp.concatenate([jnp.zeros((1, d, r_mv), F32), w_in_vres], axis=0),
         jnp.zeros((depth, d, pad), F32)], axis=-1)
    mu_all = jnp.concatenate(
        [mu_shift, jnp.concatenate([jnp.zeros((1, r_mv), F32), mu_shift_vres], axis=0), jnp.zeros((depth, pad), F32)],
        axis=-1)
    state_rwkv_t = jnp.transpose(state_rwkv, (0, 2, 3, 4, 1))

    h = jnp.concatenate([x_prompt.reshape(mp, d), x_sample.reshape(bs, d)], axis=0)
    vfirst_p = vfirst_s = None
    ret_p, rwkv_p, shift_p, shift_s = [], [], [], []
    ret_s = rwkv_s = None
    for l in range(depth):
        lp = {
            "mu": mu_all[l].reshape(1, -1),
            "w0": rwkv_w0[l].reshape(1, -1), "w_up": _pad_rows(rwkv_w_up[l], 0, LORA_PAD),
            "a0": rwkv_a0[l].reshape(1, -1), "a_up": _pad_rows(rwkv_a_up[l], o_a, LORA_PAD),
            "g_up": _pad_rows(rwkv_g_up[l], o_g, LORA_PAD),
            "k_k": rwkv_k_k[l].reshape(1, -1), "k_a": rwkv_k_a[l].reshape(1, -1),
            "r_k": rwkv_r_k[l].reshape(1, -1),
        }
        if l > 0:
            lp["v0"] = rwkv_v0[l - 1].reshape(1, -1)
            lp["v_up"] = _pad_rows(rwkv_v_up[l - 1], o_v, LORA_PAD)
        lnw, lnb = rwkv_ln_w[l].reshape(1, -1), rwkv_ln_b[l].reshape(1, -1)
        ret_lnw = ret_ln_w[l].reshape(1, -1)

        xn = _rmsnorm(h, g_attn[l], BF16)
        proj = _matmul([(xn, w_in_t, 0)], l, main_cols, F32, transposed=True, tn=WIDE_COL_TILE, name="in_proj")
        proj_lora = _matmul([(xn, w_lora, 0)], l, LORA_PAD, F32, name="in_proj_lora")
        h_last = jnp.concatenate([h[seq - 1:mp:seq], h[mp:]], axis=0)
        xn_last = _rmsnorm(h_last, g_attn[l], F32)
        shift_p.append(xn_last[:batch])
        shift_s.append(xn_last[batch:])
        prev_tok = _bf(state_shift[l])
        prev_s = (_matmul([(prev_tok, w_in_t, 0)], l, 3 * rw_w, F32, col_start=ret_cols, transposed=True,
                          name="prev_proj"),
                  _matmul([(prev_tok, w_lora, 0)], l, LORA_PAD, F32, name="prev_proj_lora"))

        feats_p = _rwkv_prep(proj, proj_lora, None, lp, vfirst_p, row_start=0, rows=mp, seq=seq,
                             ret_cols=ret_cols, width=rw_w)
        feats_s = _rwkv_prep(proj, proj_lora, prev_s, lp, vfirst_s, row_start=mp, rows=bs, seq=1,
                             ret_cols=ret_cols, width=rw_w)
        if l == 0:
            vfirst_p, vfirst_s = feats_p[10], feats_s[3]

        y_ret, s_ret_p = _ret_prompt(proj, cos_p, sin_p, lg, ret_lnw, batch=batch, seq=seq, width=ret_w,
                                     total_rows=mp + bs)
        y_ret, ret_s = _ret_sample(proj, state_ret, l, y_ret, ret_s, cos_s, sin_s, lg, ret_lnw,
                                   row_start=mp, width=ret_w)
        y_rw, s_rw_p = _rwkv_scan(feats_p, lnw, lnb, batch=batch, seq=seq, width=rw_w, total_rows=mp + bs)
        y_rw, rwkv_s = _rwkv_sample(feats_s, state_rwkv_t, l, y_rw, rwkv_s, lp["r_k"], lnw, lnb,
                                    row_start=mp, width=rw_w)
        ret_p.append(s_ret_p)
        rwkv_p.append(s_rw_p)

        h = _matmul([(y_ret, w_out, 0), (y_rw, w_out, ret_w)], l, d, F32, res=h, tn=WIDE_COL_TILE,
                    tm=_pick(mp + bs, ROW_TILES[1:]), name="out_proj")
        hn = _rmsnorm(h, g_ffn[l], BF16)
        act = _matmul([(hn, w_up, 0)], l, w_up.shape[2], BF16, gate=[w_gate], name="swiglu")
        h = _matmul([(act, w_down, 0)], l, d, F32, res=h, tm=_pick(mp + bs, FFN_DOWN_ROW_TILES), name="ffn_down")

    y_p = _rmsnorm(h, g_final, F32, row_start=0, rows=mp).reshape(batch, seq, d)
    y_s = _rmsnorm(h, g_final, F32, row_start=mp, rows=bs, tm=_pick(bs, (128, 64, 32, 16, 8))).reshape(bs, 1, d)
    return (y_p, y_s, jnp.stack(ret_p), jnp.stack(rwkv_p), jnp.stack(shift_p),
            ret_s, jnp.transpose(rwkv_s, (0, 4, 1, 2, 3)), jnp.stack(shift_s))
```

```python
import functools

import jax
import jax.numpy as jnp
from jax import lax
from jax.experimental import pallas as pl
from jax.experimental.pallas import tpu as pltpu

F32 = jnp.float32
BF16 = jnp.bfloat16

PAST_LEN = 16384
ROPE_BASE = 10000.0
RET_HEAD_DIM = 128
RET_CHUNK = 128
RWKV_HEAD_DIM = 64
RWKV_CHUNK = 64
NORM_EPS = 1e-6
RET_LN_EPS = 1e-5
RWKV_LN_EPS = 64e-5

LANES = 128
SUBLANES = 8
LORA_PAD = 512
VMEM_LIMIT_BYTES = 56 * 1024 * 1024
ROW_TILES = (1664, 832, 640, 512, 256, 128, 64, 32, 16)
FFN_DOWN_ROW_TILES = ROW_TILES[1:]
WIDE_COL_TILE = 1024
STEP_TOKENS = 8


def _params(*sem):
    return pltpu.CompilerParams(dimension_semantics=sem, vmem_limit_bytes=VMEM_LIMIT_BYTES)


def _pick(n, candidates):
    for c in candidates:
        if n % c == 0:
            return c
    raise ValueError(f"no tile in {candidates} divides {n}")


def _dot(a, b):
    return jnp.dot(a, b, preferred_element_type=F32)


def _dot_nt(a, b):
    return lax.dot_general(a, b, (((1,), (1,)), ((), ())), preferred_element_type=F32)


def _dot_tn(a, b):
    return lax.dot_general(a, b, (((0,), (0,)), ((), ())), preferred_element_type=F32)


def _bf(x):
    return x.astype(BF16)


def _split2(x):
    hi = _bf(x)
    return hi, _bf(x - hi.astype(F32))


def _rmsnorm_kernel(x_ref, g_ref, o_ref):
    x = x_ref[...]
    y = x * lax.rsqrt(jnp.mean(x * x, axis=-1, keepdims=True) + NORM_EPS)
    o_ref[...] = (y * g_ref[...]).astype(o_ref.dtype)


def _rmsnorm(x, g, out_dtype, *, row_start=0, rows=None, tm=None):
    m, d = x.shape
    rows = m if rows is None else rows
    if tm is None:
        tm = rows if rows <= 640 else _pick(rows, (640, 512, 256, 128, 64, 32, 16))
    assert row_start % tm == 0 and rows % tm == 0
    off = row_start // tm
    return pl.pallas_call(
        _rmsnorm_kernel,
        grid=(rows // tm,),
        in_specs=[pl.BlockSpec((tm, d), lambda i: (i + off, 0)), pl.BlockSpec((1, d), lambda i: (0, 0))],
        out_specs=pl.BlockSpec((tm, d), lambda i: (i, 0)),
        out_shape=jax.ShapeDtypeStruct((rows, d), out_dtype),
        compiler_params=_params("parallel"),
        name="rmsnorm",
    )(x, g.reshape(1, d))


def _mm_kernel(*refs, n_pairs, has_res, transposed, swiglu, row_splits):
    n_w = 2 * n_pairs if swiglu else n_pairs
    a_refs, w_refs = refs[:n_pairs], refs[n_pairs:n_pairs + n_w]
    w_scr = refs[len(refs) - n_w:]
    o_ref = refs[len(refs) - n_w - 1]

    @pl.when(pl.program_id(1) == 0)
    def _():
        for w_ref, scr in zip(w_refs, w_scr):
            scr[...] = w_ref[...].astype(BF16)

    def product(scrs, rows):
        acc = None
        for a_ref, scr in zip(a_refs, scrs):
            d = _dot_nt(a_ref[rows, :], scr[...]) if transposed else _dot(a_ref[rows, :], scr[...])
            acc = d if acc is None else acc + d
        return acc

    tm = o_ref.shape[0]
    sub = tm // row_splits
    for r in range(row_splits):
        rows = slice(r * sub, (r + 1) * sub)
        acc = product(w_scr[:n_pairs], rows)
        if swiglu:
            acc = acc * jax.nn.sigmoid(acc) * product(w_scr[n_pairs:], rows)
        if has_res:
            acc = acc + refs[n_pairs + n_w][rows, :]
        o_ref[rows, :] = acc.astype(o_ref.dtype)


def _matmul(pairs, layer, n, out_dtype, *, res=None, col_start=0, transposed=False, gate=None, tm=None, tn=512,
            name="matmul"):
    m = pairs[0][0].shape[0]
    tm = _pick(m, ROW_TILES) if tm is None else tm
    assert col_start % tn == 0 and n % tn == 0 and m % tm == 0
    coff = col_start // tn
    in_specs, args, scratch = [], [], []
    for a, _, _ in pairs:
        in_specs.append(pl.BlockSpec((tm, a.shape[1]), lambda j, i: (i, 0)))
        args.append(a)
    weights = [(a.shape[1], g, k_start) for (a, _, k_start), g in zip(pairs, gate or [])]
    weights += [(a.shape[1], w, k_start) for a, w, k_start in pairs]
    for k, w, k_start in weights:
        assert k_start % k == 0
        kb = k_start // k
        if transposed:
            in_specs.append(pl.BlockSpec((None, tn, k), lambda j, i, kb=kb: (layer, j + coff, kb)))
            scratch.append(pltpu.VMEM((tn, k), BF16))
        else:
            in_specs.append(pl.BlockSpec((None, k, tn), lambda j, i, kb=kb: (layer, kb, j + coff)))
            scratch.append(pltpu.VMEM((k, tn), BF16))
        args.append(w)
    if res is not None:
        in_specs.append(pl.BlockSpec((tm, tn), lambda j, i: (i, j)))
        args.append(res)
    return pl.pallas_call(
        functools.partial(_mm_kernel, n_pairs=len(pairs), has_res=res is not None, transposed=transposed,
                          swiglu=gate is not None, row_splits=_pick(tm // 16, (4, 2, 1))),
        grid=(n // tn, m // tm),
        in_specs=in_specs,
        out_specs=pl.BlockSpec((tm, tn), lambda j, i: (i, j)),
        out_shape=jax.ShapeDtypeStruct((m, n), out_dtype),
        scratch_shapes=scratch,
        compiler_params=_params("parallel", "arbitrary"),
        name=name,
    )(*args)


def _rope(x, cos, sin_signed):
    return x * cos + pltpu.roll(x, RET_HEAD_DIM // 2, 1) * sin_signed


def _head_norm(y, eps):
    mu = jnp.mean(y, axis=-1, keepdims=True)
    yc = y - mu
    var = jnp.mean(yc * yc, axis=-1, keepdims=True)
    return yc * lax.rsqrt(var + eps)


def _ret_prompt_kernel(q_ref, k_ref, v_ref, g_ref, cos_ref, sin_ref, lg_ref, lnw_ref, y_ref, s_ref, *,
                       n_chunks, group):
    c, d = RET_CHUNK, RET_HEAD_DIM
    lg = lg_ref[0]
    row = lax.broadcasted_iota(jnp.int32, (c, c), 0)
    col = lax.broadcasted_iota(jnp.int32, (c, c), 1)
    diff = (row - col).astype(F32)
    decay_in = jnp.where(diff >= 0, jnp.exp(lg * jnp.maximum(diff, 0.0)), 0.0)
    ridx = lax.broadcasted_iota(jnp.int32, (c, d), 0).astype(F32)
    tail = jnp.exp((c - 1 - ridx) * lg)
    head = jnp.exp((ridx + 1) * lg)
    chunk_decay = jnp.exp(c * lg)
    lnw = lnw_ref[...]

    def body(gi, s):
        rows = [pl.ds(pl.multiple_of((gi * group + j) * c, c), c) for j in range(group)]
        cs = [(cos_ref[r, :], sin_ref[r, :]) for r in rows]
        q = [_rope(q_ref[r, :], co, si) for r, (co, si) in zip(rows, cs)]
        k = [_rope(k_ref[r, :], co, si) * (d ** -0.5) for r, (co, si) in zip(rows, cs)]
        vb = [_bf(v_ref[r, :]) for r in rows]
        scores = [_dot_nt(_bf(qj), _bf(kj)) * decay_in for qj, kj in zip(q, k)]
        kv = [_dot_tn(_bf(kj * tail), vj) for kj, vj in zip(k, vb)]
        states = [s]
        for j in range(group):
            states.append(states[j] * chunk_decay + kv[j])
        y = [_dot(_bf(sc), vj) + _dot(_bf(qj * head), _bf(sj))
             for sc, vj, qj, sj in zip(scores, vb, q, states)]
        for r, yj in zip(rows, y):
            g = g_ref[r, :]
            y_ref[r, :] = (g * jax.nn.sigmoid(g) * (_head_norm(yj, RET_LN_EPS) * lnw)).astype(y_ref.dtype)
        return states[group]

    s_ref[0, 0] = lax.fori_loop(0, n_chunks // group, body, jnp.zeros((d, d), F32))


def _ret_prompt(proj, cos, sin, lg, ln_w, *, batch, seq, width, total_rows):
    heads = width // RET_HEAD_DIM
    assert seq % RET_CHUNK == 0 and RET_CHUNK == LANES
    n_chunks = seq // RET_CHUNK

    def col(which):
        return pl.BlockSpec((seq, RET_HEAD_DIM), lambda b, h: (b, which * heads + h))

    return pl.pallas_call(
        functools.partial(_ret_prompt_kernel, n_chunks=n_chunks, group=_pick(n_chunks, (8, 4, 2, 1))),
        grid=(batch, heads),
        in_specs=[col(0), col(1), col(2), col(3),
                  pl.BlockSpec((seq, RET_HEAD_DIM), lambda b, h: (0, 0)),
                  pl.BlockSpec((seq, RET_HEAD_DIM), lambda b, h: (0, 0)),
                  pl.BlockSpec((1, 1, LANES), lambda b, h: (h, 0, 0)),
                  pl.BlockSpec((1, RET_HEAD_DIM), lambda b, h: (0, h))],
        out_specs=[pl.BlockSpec((seq, RET_HEAD_DIM), lambda b, h: (b, h)),
                   pl.BlockSpec((1, 1, RET_HEAD_DIM, RET_HEAD_DIM), lambda b, h: (b, h, 0, 0))],
        out_shape=[jax.ShapeDtypeStruct((total_rows, width), BF16),
                   jax.ShapeDtypeStruct((batch, heads, RET_HEAD_DIM, RET_HEAD_DIM), F32)],
        compiler_params=_params("parallel", "parallel"),
        name="ret_prompt",
    )(proj, proj, proj, proj, cos, sin, lg, ln_w)


def _ret_sample_kernel(q_ref, k_ref, v_ref, g_ref, s_ref, cos_ref, sin_ref, lg_ref, lnw_ref, *rest):
    y_ref, o_ref, q_scr, kp_scr, v_scr, y_scr = rest[-6:]
    d = RET_HEAD_DIM
    tt, width = q_ref.shape
    heads = width // d
    cos, sin = cos_ref[...], sin_ref[...]
    v_scr[:, 0, :] = v_ref[...]
    for h in range(heads):
        ln = slice(h * d, (h + 1) * d)
        q_scr[:, 0, ln] = _rope(q_ref[:, ln], cos, sin)
        k = _rope(k_ref[:, ln], cos, sin) * (d ** -0.5)
        k_hi = _bf(k).astype(F32)
        kp_scr[:, 0, 2 * h * d:(2 * h + 1) * d] = k_hi
        kp_scr[:, 0, (2 * h + 1) * d:(2 * h + 2) * d] = k - k_hi
    row = lax.broadcasted_iota(jnp.int32, (d, 2 * d), 0)
    col = lax.broadcasted_iota(jnp.int32, (d, 2 * d), 1)
    eye2 = jnp.where((col == row) | (col == row + d), 1.0, 0.0).astype(BF16)

    def body(ti, carry):
        th = [(2 * ti + j, h) for j in range(2) for h in range(heads)]
        lns = [slice(h * d, (h + 1) * d) for _, h in th]
        kp = [_bf(jnp.broadcast_to(kp_scr[t, :, 2 * h * d:(2 * h + 2) * d], (d, 2 * d))) for t, h in th]
        k_col = [_dot_nt(eye2, x) for x in kp]
        s_new = [s_ref[t, h] * jnp.exp(lg_ref[h]) + kc * v_scr[t, :, ln] for (t, h), kc, ln in zip(th, k_col, lns)]
        for (t, h), s in zip(th, s_new):
            o_ref[t, h] = s
        y = [_dot(_bf(jnp.broadcast_to(q_scr[t, :, ln], (SUBLANES, d))), _bf(s))
             for (t, h), s, ln in zip(th, s_new, lns)]
        for (t, h), yi, ln in zip(th, y, lns):
            y_scr[t, :, ln] = yi[0:1]
        return carry

    assert tt % 2 == 0
    lax.fori_loop(0, tt // 2, body, 0)
    for h in range(heads):
        ln = slice(h * d, (h + 1) * d)
        g = g_ref[:, ln]
        yn = _head_norm(y_scr[:, 0, ln], RET_LN_EPS) * lnw_ref[:, ln]
        y_ref[:, ln] = (g * jax.nn.sigmoid(g) * yn).astype(y_ref.dtype)


def _in_place(bufs, n_inputs):
    specs, args, aliases = [], [], {}
    for out_idx, buf in enumerate(bufs):
        if buf is not None:
            aliases[n_inputs + len(args)] = out_idx
            specs.append(pl.BlockSpec(memory_space=pl.ANY))
            args.append(buf)
    return specs, args, aliases


def _ret_sample(proj, states, layer, y_buf, new_states, cos, sin, lg, ln_w, *, row_start, width):
    bs, heads = states.shape[1], states.shape[2]
    tt = STEP_TOKENS
    assert row_start % tt == 0 and bs % tt == 0
    roff = row_start // tt

    def col(which):
        return pl.BlockSpec((tt, width), lambda i: (i + roff, which))

    st = pl.BlockSpec((None, tt, heads, RET_HEAD_DIM, RET_HEAD_DIM), lambda i: (layer, i, 0, 0, 0))
    in_specs = [col(0), col(1), col(2), col(3), st,
                pl.BlockSpec((1, RET_HEAD_DIM), lambda i: (0, 0)),
                pl.BlockSpec((1, RET_HEAD_DIM), lambda i: (0, 0)),
                pl.BlockSpec((heads, 1, LANES), lambda i: (0, 0, 0)),
                pl.BlockSpec((1, width), lambda i: (0, 0))]
    alias_specs, alias_args, aliases = _in_place([y_buf, new_states], len(in_specs))
    return pl.pallas_call(
        _ret_sample_kernel,
        grid=(bs // tt,),
        in_specs=in_specs + alias_specs,
        out_specs=[pl.BlockSpec((tt, width), lambda i: (i + roff, 0)), st],
        out_shape=[jax.ShapeDtypeStruct(y_buf.shape, y_buf.dtype), jax.ShapeDtypeStruct(states.shape, states.dtype)],
        scratch_shapes=[pltpu.VMEM((tt, 1, width), F32), pltpu.VMEM((tt, 1, 2 * width), F32),
                        pltpu.VMEM((tt, 1, width), F32), pltpu.VMEM((tt, 1, width), F32)],
        input_output_aliases=aliases,
        compiler_params=_params("parallel"),
        name="ret_sample",
    )(proj, proj, proj, proj, states, cos, sin, lg, ln_w, *alias_args)


def _softplus(x):
    return jnp.maximum(x, 0.0) + jnp.log1p(jnp.exp(-jnp.abs(x)))


def _group_matrix():
    r = lax.broadcasted_iota(jnp.int32, (LANES, LANES), 0) // RWKV_HEAD_DIM
    c = lax.broadcasted_iota(jnp.int32, (LANES, LANES), 1) // RWKV_HEAD_DIM
    return jnp.where(r == c, 1.0, 0.0).astype(BF16)


def _group_sum(x, gmat):
    outs = []
    for j in range(x.shape[1] // LANES):
        hi, lo = _split2(x[:, j * LANES:(j + 1) * LANES])
        outs.append(_dot(hi, gmat) + _dot(lo, gmat))
    return outs[0] if len(outs) == 1 else jnp.concatenate(outs, axis=1)


def _cumsum_rows(x, tri):
    hi = _bf(x)
    r1 = x - hi.astype(F32)
    mid = _bf(r1)
    lo = _bf(r1 - mid.astype(F32))
    return _dot(tri, hi) + _dot(tri, mid) + _dot(tri, lo)


def _rwkv_prep_kernel(*refs, seq_mode, has_vres, tiles_per_seq, width):
    it = iter(refs)
    cur = [next(it) for _ in range(4)]
    prev = [next(it) for _ in range(4)]
    mu_ref, w0_ref, wup_ref, a0_ref, aup_ref, gup_ref, kk_ref, ka_ref, rk_ref = (next(it) for _ in range(9))
    if has_vres:
        v0_ref, vup_ref, vfirst_ref = (next(it) for _ in range(3))
    outs = list(it)
    i = pl.program_id(0)

    def shifted(idx, mu):
        p = cur[idx][...]
        if seq_mode:
            carry = jnp.where(i % tiles_per_seq == 0, 0.0, prev[idx][SUBLANES - 1:SUBLANES, :])
            rowi = lax.broadcasted_iota(jnp.int32, p.shape, 0)
            p_prev = jnp.where(rowi == 0, carry, pltpu.roll(p, 1, 0))
        else:
            p_prev = prev[idx][...]
        return p + (p_prev - p) * mu

    w = width
    r = shifted(0, mu_ref[:, 0:w])
    k = shifted(1, mu_ref[:, w:2 * w])
    v = shifted(2, mu_ref[:, 2 * w:3 * w])
    lo = shifted(3, mu_ref[:, 3 * w:])
    lob = _bf(lo)
    w_raw = -_softplus(-(w0_ref[...] + _dot(_bf(jnp.tanh(lo)), wup_ref[...]))) - 0.5
    lw = -jnp.exp(w_raw)
    a = jax.nn.sigmoid(a0_ref[...] + _dot(lob, aup_ref[...]))
    g = _dot(_bf(jax.nn.sigmoid(lo)), gup_ref[...])
    if has_vres:
        v = v + (vfirst_ref[...] - v) * jax.nn.sigmoid(v0_ref[...] + _dot(lob, vup_ref[...]))
    k2 = k * (1.0 + (a - 1.0) * ka_ref[...])
    kk = k * kk_ref[...]
    gmat = _group_matrix()
    kk = kk / jnp.maximum(jnp.sqrt(_group_sum(kk * kk, gmat)), 1e-12)
    b = kk * a
    if not seq_mode:
        for o_ref, val in zip(outs, (r, lw, k2, v, kk, b, g)):
            o_ref[...] = val
        return

    tp, c = r.shape[0], RWKV_CHUNK
    row = lax.broadcasted_iota(jnp.int32, (tp, tp), 0)
    col = lax.broadcasted_iota(jnp.int32, (tp, tp), 1)
    tri = jnp.where(row >= col, jnp.where(row // c == col // c, 1.0, 0.0), 0.0).astype(BF16)
    cum = _cumsum_rows(lw, tri)
    last = [cum[j * c + c - 1:j * c + c, :] for j in range(tp // c)]
    tot = jnp.concatenate([jnp.broadcast_to(x, (c, w)) for x in last], axis=0)
    e_in = jnp.exp(-cum)
    e_out = jnp.exp(tot - cum)
    bonus = _group_sum(r * k2 * rk_ref[...], gmat) * v
    vals = (kk * jnp.exp(cum - lw), r * jnp.exp(cum), b * e_in, k2 * e_in, b * e_out, k2 * e_out, v, bonus, g)
    for o_ref, val in zip(outs, vals):
        o_ref[...] = val.astype(o_ref.dtype)
    outs[9][...] = jnp.concatenate([jnp.broadcast_to(jnp.exp(x), (SUBLANES, w)) for x in last], axis=0)
    if not has_vres:
        outs[10][...] = v


def _rwkv_prep(proj, proj_lora, prev, lp, vfirst, *, row_start, rows, seq, ret_cols, width):
    seq_mode = prev is None
    has_vres = vfirst is not None
    tp = _pick(seq, (256, 128, 64)) if seq_mode else _pick(rows, (128, 64, 32, 16, 8))
    assert row_start % tp == 0 and ret_cols % width == 0
    roff = row_start // tp
    cb = ret_cols // width
    cur = ((proj, width, cb), (proj, width, cb + 1), (proj, width, cb + 2), (proj_lora, LORA_PAD, 0))
    in_specs, args = [], []
    for arr, wd, blk in cur:
        in_specs.append(pl.BlockSpec((tp, wd), lambda i, blk=blk: (i + roff, blk)))
        args.append(arr)
    if seq_mode:
        for arr, wd, blk in cur:
            in_specs.append(pl.BlockSpec(
                (SUBLANES, wd), lambda i, blk=blk: (jnp.maximum((i + roff) * (tp // SUBLANES) - 1, 0), blk)))
            args.append(arr)
    else:
        for arr, wd, blk in ((prev[0], width, 0), (prev[0], width, 1), (prev[0], width, 2), (prev[1], LORA_PAD, 0)):
            in_specs.append(pl.BlockSpec((tp, wd), lambda i, blk=blk: (i, blk)))
            args.append(arr)

    def full(x):
        return pl.BlockSpec(x.shape, lambda i: (0, 0))

    names = ["mu", "w0", "w_up", "a0", "a_up", "g_up", "k_k", "k_a", "r_k"] + (["v0", "v_up"] if has_vres else [])
    for nm in names:
        in_specs.append(full(lp[nm]))
        args.append(lp[nm])
    if has_vres:
        in_specs.append(pl.BlockSpec((tp, width), lambda i: (i, 0)))
        args.append(vfirst)
    tok = pl.BlockSpec((tp, width), lambda i: (i, 0))
    if seq_mode:
        assert tp % RWKV_CHUNK == 0
        per_chunk = pl.BlockSpec((tp // RWKV_CHUNK * SUBLANES, width), lambda i: (i, 0))
        out_specs = [tok] * 9 + [per_chunk] + ([] if has_vres else [tok])
        out_shape = ([jax.ShapeDtypeStruct((rows, width), BF16)] * 9
                     + [jax.ShapeDtypeStruct((rows // RWKV_CHUNK * SUBLANES, width), F32)]
                     + ([] if has_vres else [jax.ShapeDtypeStruct((rows, width), F32)]))
    else:
        out_specs = [tok] * 7
        out_shape = [jax.ShapeDtypeStruct((rows, width), F32)] * 7
    return pl.pallas_call(
        functools.partial(_rwkv_prep_kernel, seq_mode=seq_mode, has_vres=has_vres,
                          tiles_per_seq=max(seq // tp, 1), width=width),
        grid=(rows // tp,),
        in_specs=in_specs,
        out_specs=out_specs,
        out_shape=out_shape,
        compiler_params=_params("parallel"),
        name="rwkv_prep",
    )(*args)


def _rwkv_scan_kernel(khat_ref, rhat_ref, btil_ref, ktil_ref, bbar_ref, kbar_ref, v_ref, bonus_ref, g_ref,
                      dtot_ref, lnw_ref, lnb_ref, y_ref, s_ref,
                      r_scr, o0_scr, m_scr, n_scr, o_scr, *, n_chunks, chunk_group, post_rows):
    c, n = RWKV_CHUNK, RWKV_HEAD_DIM
    pairs = khat_ref.shape[1] // LANES
    row = lax.broadcasted_iota(jnp.int32, (c, c), 0)
    col = lax.broadcasted_iota(jnp.int32, (c, c), 1)
    strict, incl = row > col, row >= col
    eye = jnp.where(row == col, 1.0, 0.0)
    first = lax.broadcasted_iota(jnp.int32, (c, LANES), 1) < n
    first_b = jnp.where(first, 1.0, 0.0).astype(BF16)
    second_b = jnp.where(first, 0.0, 1.0).astype(BF16)
    vrow = lax.broadcasted_iota(jnp.int32, (LANES, LANES), 0) < n
    kcol = lax.broadcasted_iota(jnp.int32, (LANES, LANES), 1) < n
    same_head = vrow == kcol

    def lanes(p):
        return slice(p * LANES, (p + 1) * LANES)

    def pass1(gi, carry):
        inst = [(gi * chunk_group + cj, p) for cj in range(chunk_group) for p in range(pairs)]
        rows = [pl.ds(pl.multiple_of(ci * c, c), c) for ci, _ in inst]
        at = [(rows[i], lanes(p)) for i, (_, p) in enumerate(inst)]
        khat = [khat_ref[a] for a in at]
        rhat = [rhat_ref[a] for a in at]
        v2 = [v_ref[a] for a in at]
        x = [jnp.concatenate([kh * first_b, kh * second_b, rh * first_b, rh * second_b], axis=0)
             for kh, rh in zip(khat, rhat)]
        abk = [_dot_nt(xi, jnp.concatenate([btil_ref[a], ktil_ref[a]], axis=0)) for xi, a in zip(x, at)]
        ab = [z[:, :c] for z in abk]
        ak = [z[:, c:] for z in abk]
        heads2 = [(i, hh) for i in range(len(inst)) for hh in range(2)]
        pw = [-jnp.where(strict, ab[i][hh * c:(hh + 1) * c], 0.0) for i, hh in heads2]
        a_k = [_bf(jnp.where(strict, ak[i][hh * c:(hh + 1) * c], 0.0)) for i, hh in heads2]
        a_rk = [_bf(jnp.where(incl, ak[i][(2 + hh) * c:(3 + hh) * c], 0.0)) for i, hh in heads2]
        a_rb = [_bf(jnp.where(incl, ab[i][(2 + hh) * c:(3 + hh) * c], 0.0)) for i, hh in heads2]
        akv = [_dot(a_k[j], v2[i]) for j, (i, _) in enumerate(heads2)]
        arkv = [_dot(a_rk[j], v2[i]) for j, (i, _) in enumerate(heads2)]
        akv_b = [_bf(jnp.where(first, akv[2 * i], akv[2 * i + 1])) for i in range(len(inst))]
        tinv = [eye + q for q in pw]
        for _ in range(c.bit_length() - 2):
            pwb = [_bf(q) for q in pw]
            pw = [_dot(q, q) for q in pwb]
            tinv = [t + _dot(_bf(t), _bf(q)) for t, q in zip(tinv, pw)]
        tinv = [_bf(t) for t in tinv]
        wt = [_dot(tinv[j], khat[i]) for j, (i, _) in enumerate(heads2)]
        u0 = [_dot(tinv[j], akv_b[i]) for j, (i, _) in enumerate(heads2)]
        wt_b = [_bf(jnp.where(first, wt[2 * i], wt[2 * i + 1])) for i in range(len(inst))]
        u0_b = [_bf(jnp.where(first, u0[2 * i], u0[2 * i + 1])) for i in range(len(inst))]
        bbar = [bbar_ref[a] for a in at]
        m = [_dot_tn(wt_b[i], bbar[i]) for i in range(len(inst))]
        nn = [_dot_tn(jnp.concatenate([v2[i], -u0_b[i]], axis=0),
                      jnp.concatenate([kbar_ref[at[i]], bbar[i]], axis=0)) for i in range(len(inst))]
        rw = [_dot(a_rb[j], wt_b[i]) for j, (i, _) in enumerate(heads2)]
        ru = [_dot(a_rb[j], u0_b[i]) for j, (i, _) in enumerate(heads2)]
        for i, (ci, p) in enumerate(inst):
            m_scr[ci, p] = _bf(jnp.where(same_head, -m[i], 0.0))
            n_scr[ci, p] = jnp.where(same_head, nn[i], 0.0)
            r_scr[at[i]] = _bf(rhat[i].astype(F32) - jnp.where(first, rw[2 * i], rw[2 * i + 1]))
            o0_scr[at[i]] = jnp.where(first, arkv[2 * i] - ru[2 * i], arkv[2 * i + 1] - ru[2 * i + 1])
        return carry

    lax.fori_loop(0, n_chunks // chunk_group, pass1, 0)

    def pass2(ci, states):
        rows = pl.ds(pl.multiple_of(ci * c, c), c)
        drows = pl.ds(pl.multiple_of(ci * SUBLANES, SUBLANES), SUBLANES)
        sb = [_bf(s) for s in states]
        ds = [_dot(sb[p], m_scr[ci, p]) for p in range(pairs)]
        for p in range(pairs):
            at = (rows, lanes(p))
            o_scr[at] = _dot_nt(r_scr[at], sb[p]) + o0_scr[at]
        return tuple(states[p] * dtot_ref[drows, lanes(p)][0:1] + ds[p] + n_scr[ci, p] for p in range(pairs))

    final = lax.fori_loop(0, n_chunks, pass2, tuple(jnp.zeros((LANES, LANES), F32) for _ in range(pairs)))
    for p in range(pairs):
        s_ref[0, 2 * p] = final[p][:n, :n]
        s_ref[0, 2 * p + 1] = final[p][n:, n:]

    gmat = _group_matrix()

    def post(ti, carry):
        rows = pl.ds(pl.multiple_of(ti * post_rows, post_rows), post_rows)
        o = o_scr[rows, :]
        oc = o - _group_sum(o, gmat) * (1.0 / n)
        var = _group_sum(oc * oc, gmat) * (1.0 / n)
        y = oc * lax.rsqrt(var + RWKV_LN_EPS) * lnw_ref[...] + lnb_ref[...]
        y_ref[rows, :] = ((y + bonus_ref[rows, :].astype(F32)) * g_ref[rows, :].astype(F32)).astype(y_ref.dtype)
        return carry

    lax.fori_loop(0, (n_chunks * c) // post_rows, post, 0)


def _rwkv_scan(feats, lnw, lnb, *, batch, seq, width, total_rows):
    heads = width // RWKV_HEAD_DIM
    blk = _pick(width, (2 * LANES, LANES))
    hb = blk // RWKV_HEAD_DIM
    c = RWKV_CHUNK
    assert seq % c == 0
    n_chunks = seq // c
    tok = pl.BlockSpec((seq, blk), lambda b, h: (b, h))
    par = pl.BlockSpec((1, blk), lambda b, h: (0, h))
    return pl.pallas_call(
        functools.partial(_rwkv_scan_kernel, n_chunks=n_chunks, chunk_group=_pick(n_chunks, (8, 4, 2, 1)),
                          post_rows=_pick(seq, (1024, 512, 256, 128, 64))),
        grid=(batch, width // blk),
        in_specs=[tok] * 9 + [pl.BlockSpec((n_chunks * SUBLANES, blk), lambda b, h: (b, h)), par, par],
        out_specs=[tok, pl.BlockSpec((1, hb, RWKV_HEAD_DIM, RWKV_HEAD_DIM), lambda b, h: (b, h, 0, 0))],
        out_shape=[jax.ShapeDtypeStruct((total_rows, width), BF16),
                   jax.ShapeDtypeStruct((batch, heads, RWKV_HEAD_DIM, RWKV_HEAD_DIM), F32)],
        scratch_shapes=[pltpu.VMEM((seq, blk), BF16), pltpu.VMEM((seq, blk), F32),
                        pltpu.VMEM((n_chunks, blk // LANES, LANES, LANES), BF16),
                        pltpu.VMEM((n_chunks, blk // LANES, LANES, LANES), F32),
                        pltpu.VMEM((seq, blk), F32)],
        compiler_params=_params("parallel", "parallel"),
        name="rwkv_scan",
    )(*feats[:10], lnw, lnb)


def _rwkv_sample_kernel(r_ref, lw_ref, k_ref, v_ref, kk_ref, b_ref, g_ref, s_ref, rk_ref, lnw_ref, lnb_ref, *rest):
    y_ref, o_ref, v_scr, out_scr = rest[-4:]
    n = RWKV_HEAD_DIM
    tokens = r_ref.shape[0]

    def chan(ref):
        return ref[...].T

    def col(ref):
        return jnp.broadcast_to(ref[...], (tokens, LANES)).T

    r, k2, v, kk, b, g = chan(r_ref), chan(k_ref), chan(v_ref), chan(kk_ref), chan(b_ref), chan(g_ref)
    w = jnp.exp(chan(lw_ref))
    v_scr[:, 0, :] = v
    for hh in range(s_ref.shape[0]):
        ch = slice(hh * n, (hh + 1) * n)
        kk_h, w_h, b_h, k_h, r_h = kk[ch], w[ch], b[ch], k2[ch], r[ch]

        def body(vi, carry):
            s = s_ref[hh, vi]
            sa = -jnp.sum(s * kk_h, axis=0, keepdims=True)
            new = s * w_h + sa * b_h + v_scr[hh * n + vi] * k_h
            o_ref[hh, vi] = new
            out_scr[hh * n + vi] = jnp.sum(new * r_h, axis=0, keepdims=True)
            return carry

        lax.fori_loop(0, n, body, 0, unroll=4)
    o = out_scr[:, 0, :]
    rkk = r * k2 * col(rk_ref)
    yn, bonus = [], []
    for hh in range(s_ref.shape[0]):
        ch = slice(hh * n, (hh + 1) * n)
        oc = o[ch] - jnp.mean(o[ch], axis=0, keepdims=True)
        var = jnp.mean(oc * oc, axis=0, keepdims=True)
        yn.append(oc * lax.rsqrt(var + RWKV_LN_EPS))
        bonus.append(jnp.sum(rkk[ch], axis=0, keepdims=True) * v[ch])
    y = (jnp.concatenate(yn, axis=0) * col(lnw_ref) + col(lnb_ref) + jnp.concatenate(bonus, axis=0)) * g
    y_ref[...] = y.T.astype(y_ref.dtype)


def _rwkv_sample(feats, states_t, layer, y_buf, new_states, rk, lnw, lnb, *, row_start, width):
    heads, bs = states_t.shape[1], states_t.shape[4]
    hb = LANES // RWKV_HEAD_DIM
    assert row_start % bs == 0 and heads % hb == 0
    roff = row_start // bs
    tok = pl.BlockSpec((bs, LANES), lambda j: (0, j))
    par = pl.BlockSpec((1, LANES), lambda j: (0, j))
    st = pl.BlockSpec((None, hb, RWKV_HEAD_DIM, RWKV_HEAD_DIM, bs), lambda j: (layer, j, 0, 0, 0))
    in_specs = [tok] * 7 + [st] + [par] * 3
    alias_specs, alias_args, aliases = _in_place([y_buf, new_states], len(in_specs))
    return pl.pallas_call(
        _rwkv_sample_kernel,
        grid=(heads // hb,),
        in_specs=in_specs + alias_specs,
        out_specs=[pl.BlockSpec((bs, LANES), lambda j: (roff, j)), st],
        out_shape=[jax.ShapeDtypeStruct(y_buf.shape, y_buf.dtype),
                   jax.ShapeDtypeStruct(states_t.shape, states_t.dtype)],
        scratch_shapes=[pltpu.VMEM((LANES, 1, bs), F32), pltpu.VMEM((LANES, 1, bs), F32)],
        input_output_aliases=aliases,
        compiler_params=_params("parallel"),
        name="rwkv_sample",
    )(*feats, states_t, rk, lnw, lnb, *alias_args)


def _rope_tables(pos):
    half = RET_HEAD_DIM // 2
    inv_freq = ROPE_BASE ** (-jnp.arange(half, dtype=F32) / half)
    ang = pos[:, None] * inv_freq[None, :]
    cos, sin = jnp.cos(ang), jnp.sin(ang)
    return jnp.concatenate([cos, cos], axis=-1), jnp.concatenate([-sin, sin], axis=-1)


def _pad_rows(w, start, total):
    return jnp.zeros((total, w.shape[1]), BF16).at[start:start + w.shape[0]].set(_bf(w))


def kernel(x_prompt, x_sample, state_ret, state_rwkv, state_shift, w_in, w_in_vres, mu_shift, mu_shift_vres, ret_ln_w, rwkv_w0, rwkv_w_up, rwkv_a0, rwkv_a_up, rwkv_g_up, rwkv_v0, rwkv_v_up, rwkv_k_k, rwkv_k_a, rwkv_r_k, rwkv_ln_w, rwkv_ln_b, w_out, g_attn, g_ffn, w_gate, w_up, w_down, g_final):
    batch, seq, d = x_prompt.shape
    bs = x_sample.shape[0]
    assert x_sample.shape[1] == 1
    depth = w_in.shape[0]
    mp = batch * seq
    ret_w = d // 2
    rw_w = d - ret_w
    ret_cols = 4 * ret_w
    main_cols = ret_cols + 3 * rw_w
    r_decay, r_aaa, r_gate = rwkv_w_up.shape[1], rwkv_a_up.shape[1], rwkv_g_up.shape[1]
    r_mv = rwkv_v_up.shape[1]
    o_a, o_g = r_decay, r_decay + r_aaa
    o_v = o_g + r_gate
    assert o_v + r_mv <= LORA_PAD and w_in.shape[2] == main_cols + o_v
    ret_heads = ret_w // RET_HEAD_DIM

    log_gamma = jnp.log1p(-jnp.exp2(-5.0 - jnp.arange(ret_heads, dtype=F32)))
    lg = jnp.broadcast_to(log_gamma[:, None, None], (ret_heads, 1, LANES))
    cos_p, sin_p = _rope_tables(jnp.arange(seq, dtype=F32))
    cos_s, sin_s = _rope_tables(PAST_LEN + jnp.arange(1, dtype=F32))

    pad = LORA_PAD - o_v - r_mv
    w_in_t = jnp.transpose(w_in, (0, 2, 1))
    w_lora = jnp.concatenate(
        [w_in[:, :, main_cols:], jnp.concatenate([jnp.zeros((1, d, r_mv), F32), w_in_vres], axis=0),
         jnp.zeros((depth, d, pad), F32)], axis=-1)
    mu_all = jnp.concatenate(
        [mu_shift, jnp.concatenate([jnp.zeros((1, r_mv), F32), mu_shift_vres], axis=0), jnp.zeros((depth, pad), F32)],
        axis=-1)
    state_rwkv_t = jnp.transpose(state_rwkv, (0, 2, 3, 4, 1))

    h = jnp.concatenate([x_prompt.reshape(mp, d), x_sample.reshape(bs, d)], axis=0)
    vfirst_p = vfirst_s = None
    ret_p, rwkv_p, shift_p, shift_s = [], [], [], []
    ret_s = rwkv_s = None
    for l in range(depth):
        lp = {
            "mu": mu_all[l].reshape(1, -1),
            "w0": rwkv_w0[l].reshape(1, -1), "w_up": _pad_rows(rwkv_w_up[l], 0, LORA_PAD),
            "a0": rwkv_a0[l].reshape(1, -1), "a_up": _pad_rows(rwkv_a_up[l], o_a, LORA_PAD),
            "g_up": _pad_rows(rwkv_g_up[l], o_g, LORA_PAD),
            "k_k": rwkv_k_k[l].reshape(1, -1), "k_a": rwkv_k_a[l].reshape(1, -1),
            "r_k": rwkv_r_k[l].reshape(1, -1),
        }
        if l > 0:
            lp["v0"] = rwkv_v0[l - 1].reshape(1, -1)
            lp["v_up"] = _pad_rows(rwkv_v_up[l - 1], o_v, LORA_PAD)
        lnw, lnb = rwkv_ln_w[l].reshape(1, -1), rwkv_ln_b[l].reshape(1, -1)
        ret_lnw = ret_ln_w[l].reshape(1, -1)

        xn = _rmsnorm(h, g_attn[l], BF16)
        proj = _matmul([(xn, w_in_t, 0)], l, main_cols, F32, transposed=True, tn=WIDE_COL_TILE, name="in_proj")
        proj_lora = _matmul([(xn, w_lora, 0)], l, LORA_PAD, F32, name="in_proj_lora")
        h_last = jnp.concatenate([h[seq - 1:mp:seq], h[mp:]], axis=0)
        xn_last = _rmsnorm(h_last, g_attn[l], F32)
        shift_p.append(xn_last[:batch])
        shift_s.append(xn_last[batch:])
        prev_tok = _bf(state_shift[l])
        prev_s = (_matmul([(prev_tok, w_in_t, 0)], l, 3 * rw_w, F32, col_start=ret_cols, transposed=True,
                          name="prev_proj"),
                  _matmul([(prev_tok, w_lora, 0)], l, LORA_PAD, F32, name="prev_proj_lora"))

        feats_p = _rwkv_prep(proj, proj_lora, None, lp, vfirst_p, row_start=0, rows=mp, seq=seq,
                             ret_cols=ret_cols, width=rw_w)
        feats_s = _rwkv_prep(proj, proj_lora, prev_s, lp, vfirst_s, row_start=mp, rows=bs, seq=1,
                             ret_cols=ret_cols, width=rw_w)
        if l == 0:
            vfirst_p, vfirst_s = feats_p[10], feats_s[3]

        y_ret, s_ret_p = _ret_prompt(proj, cos_p, sin_p, lg, ret_lnw, batch=batch, seq=seq, width=ret_w,
                                     total_rows=mp + bs)
        y_ret, ret_s = _ret_sample(proj, state_ret, l, y_ret, ret_s, cos_s, sin_s, lg, ret_lnw,
                                   row_start=mp, width=ret_w)
        y_rw, s_rw_p = _rwkv_scan(feats_p, lnw, lnb, batch=batch, seq=seq, width=rw_w, total_rows=mp + bs)
        y_rw, rwkv_s = _rwkv_sample(feats_s, state_rwkv_t, l, y_rw, rwkv_s, lp["r_k"], lnw, lnb,
                                    row_start=mp, width=rw_w)
        ret_p.append(s_ret_p)
        rwkv_p.append(s_rw_p)

        h = _matmul([(y_ret, w_out, 0), (y_rw, w_out, ret_w)], l, d, F32, res=h, tn=WIDE_COL_TILE,
                    tm=_pick(mp + bs, ROW_TILES[1:]), name="out_proj")
        hn = _rmsnorm(h, g_ffn[l], BF16)
        act = _matmul([(hn, w_up, 0)], l, w_up.shape[2], BF16, gate=[w_gate], name="swiglu")
        h = _matmul([(act, w_down, 0)], l, d, F32, res=h, tm=_pick(mp + bs, FFN_DOWN_ROW_TILES), name="ffn_down")

    y_p = _rmsnorm(h, g_final, F32, row_start=0, rows=mp).reshape(batch, seq, d)
    y_s = _rmsnorm(h, g_final, F32, row_start=mp, rows=bs, tm=_pick(bs, (128, 64, 32, 16, 8))).reshape(bs, 1, d)
    return (y_p, y_s, jnp.stack(ret_p), jnp.stack(rwkv_p), jnp.stack(shift_p),
            ret_s, jnp.transpose(rwkv_s, (0, 4, 1, 2, 3)), jnp.stack(shift_s))
```
